```python
import functools
import jax, jax.numpy as jnp
from jax import lax
import numpy as np

D_MODEL = 4096
BATCH = 16
SEQ = 256
DEPTH = 4
DEC_BATCH = 4
DEC_SEQ = 1024
PAST_LEN = 256

GRID_W = 64
NA_HEADS = 16
NA_HEAD_DIM = 128
NA_WIDTH = NA_HEADS * NA_HEAD_DIM
NA_ROWS_MAX = 8
NA_COLS = 16
Q_BLOCK = 128
FNET_WIDTH = D_MODEL // 4
FNET_GROUPS = 4
FNET_GROUP_DIM = FNET_WIDTH // FNET_GROUPS
CONV_WIDTH = D_MODEL // 4
CONV_K = 3
N_BRANCH = 3
D_FF = 11008
N_IN = 3 * NA_WIDTH + FNET_WIDTH + 3 * CONV_WIDTH + N_BRANCH * D_MODEL
N_MOD = 6 * D_MODEL
EPS = 1e-6
NEG_INF = -1e30

kernel_name = "hybrid_natten_fnet_shortconv_dit_step"


def rms_norm(x, g):
    xf = x.astype(jnp.float32)
    y = xf * lax.rsqrt(jnp.mean(xf * xf, axis=-1, keepdims=True) + EPS)
    return (y * g.astype(jnp.float32)).astype(x.dtype)


def dwconv3(x, w, b):
    xp = jnp.pad(x, ((0, 0), (1, 1), (0, 0)))
    return xp[:, :-2] * w[0] + xp[:, 1:-1] * w[1] + xp[:, 2:] * w[2] + b


def ctx_attention(q, k, v):
    B, L, H, dh = q.shape
    nb = L // Q_BLOCK
    scale = dh ** -0.5
    qb = q.reshape(B, nb, Q_BLOCK, H, dh).transpose(1, 0, 2, 3, 4)

    def one(q_blk):
        s = jnp.einsum('bqhd,bkhd->bhqk', q_blk, k).astype(jnp.float32) * scale
        p = jax.nn.softmax(s, axis=-1).astype(v.dtype)
        return jnp.einsum('bhqk,bkhd->bqhd', p, v)

    o = lax.map(one, qb)
    return o.transpose(1, 0, 2, 3, 4).reshape(B, L, H * dh)


def na_attention(q, k, v, k_ctx, v_ctx, rpb):
    B, T, H, dh = q.shape
    rows = T // GRID_W
    kr = min(NA_ROWS_MAX, rows)
    scale = dh ** -0.5
    qg = q.reshape(B, rows, GRID_W, H, dh).transpose(1, 0, 2, 3, 4)
    kg = k.reshape(B, rows, GRID_W, H, dh)
    vg = v.reshape(B, rows, GRID_W, H, dh)
    col = jnp.arange(GRID_W)
    cs = jnp.clip(col - NA_COLS // 2, 0, GRID_W - NA_COLS)
    col_ok = (col[None, :] >= cs[:, None]) & (col[None, :] < cs[:, None] + NA_COLS)
    col_idx = jnp.clip(col[None, :] - col[:, None] + NA_COLS - 1, 0, 2 * NA_COLS - 2)
    rpb32 = rpb.astype(jnp.float32)

    def one(args):
        r, q_blk = args
        rs = jnp.clip(r - kr // 2, 0, rows - kr)
        k_blk = lax.dynamic_slice_in_dim(kg, rs, kr, axis=1).reshape(B, kr * GRID_W, H, dh)
        v_blk = lax.dynamic_slice_in_dim(vg, rs, kr, axis=1).reshape(B, kr * GRID_W, H, dh)
        row_idx = rs + jnp.arange(kr) - r + NA_ROWS_MAX - 1
        bias = rpb32[:, row_idx[None, :, None], col_idx[:, None, :]]
        bias = jnp.where(col_ok[:, None, :], bias, NEG_INF).reshape(H, GRID_W, kr * GRID_W)
        s_lat = jnp.einsum('bqhd,bkhd->bhqk', q_blk, k_blk).astype(jnp.float32) * scale + bias
        s_ctx = jnp.einsum('bqhd,bkhd->bhqk', q_blk, k_ctx).astype(jnp.float32) * scale
        p = jax.nn.softmax(jnp.concatenate([s_lat, s_ctx], axis=-1), axis=-1).astype(v.dtype)
        n_lat = kr * GRID_W
        return (jnp.einsum('bhqk,bkhd->bqhd', p[..., :n_lat], v_blk)
                + jnp.einsum('bhqk,bkhd->bqhd', p[..., n_lat:], v_ctx))

    o = lax.map(one, (jnp.arange(rows), qg))
    return o.transpose(1, 0, 2, 3, 4).reshape(B, T, H * dh)


def token_mixer(h, p, attend):
    B, T, _ = h.shape
    z = h @ p['w_in']
    cuts = [NA_WIDTH, 2 * NA_WIDTH, 3 * NA_WIDTH, 3 * NA_WIDTH + FNET_WIDTH,
            3 * NA_WIDTH + FNET_WIDTH + CONV_WIDTH, 3 * NA_WIDTH + FNET_WIDTH + 2 * CONV_WIDTH,
            3 * NA_WIDTH + FNET_WIDTH + 3 * CONV_WIDTH]
    zq, zk, zv, zf, zx, zb, zc, zg = jnp.split(z, cuts, axis=-1)
    q = zq.reshape(B, T, NA_HEADS, NA_HEAD_DIM)
    k = zk.reshape(B, T, NA_HEADS, NA_HEAD_DIM)
    v = zv.reshape(B, T, NA_HEADS, NA_HEAD_DIM)
    a = attend(q, k, v) @ p['w_na_out']
    fg = zf.reshape(B, T, FNET_GROUPS, FNET_GROUP_DIM).astype(jnp.float32)
    fr = jnp.fft.fft2(fg, axes=(1, 3), norm='ortho').real.astype(h.dtype).reshape(B, T, FNET_WIDTH)
    b = fr @ p['w_fnet_out']
    cv = zb * dwconv3(zc * zx, p['conv_w'], p['conv_b'])
    cc = cv @ p['w_conv_out']
    g = jax.nn.sigmoid(zg.astype(jnp.float32)).astype(h.dtype).reshape(B, T, N_BRANCH, D_MODEL)
    merged = g[:, :, 0] * a + g[:, :, 1] * b + g[:, :, 2] * cc
    return merged @ p['w_o'], k, v


def conv_ffn(h, p):
    gate, val = jnp.split(h @ p['w_up'], 2, axis=-1)
    gate = dwconv3(gate, p['ffn_conv_w'], p['ffn_conv_b'])
    return (jax.nn.silu(gate) * val) @ p['w_down']


def layer(x, mod, p, attend):
    sh1, sc1, g1, sh2, sc2, g2 = jnp.split(mod[:, None, :], 6, axis=-1)
    h = rms_norm(x, p['g_pre1']) * (1 + sc1) + sh1
    mix, k, v = token_mixer(h, p, attend)
    x = x + g1 * rms_norm(mix, p['g_post1'])
    h = rms_norm(x, p['g_pre2']) * (1 + sc2) + sh2
    x = x + g2 * rms_norm(conv_ffn(h, p), p['g_post2'])
    return x, k, v


def setup_inputs(seed: int = 0) -> dict:
    key = jax.random.key(seed)
    ks = jax.random.split(key, 32)
    f32 = jnp.float32

    def nrm(k, shape, scale):
        return jax.random.normal(k, shape, f32) * scale

    L = DEPTH
    return {
        'x_prompt': nrm(ks[0], (BATCH, SEQ, D_MODEL), 1.0),
        'x_sample': nrm(ks[1], (DEC_BATCH, DEC_SEQ, D_MODEL), 1.0),
        'cache_k': nrm(ks[2], (DEC_BATCH, DEPTH, PAST_LEN, NA_HEADS, NA_HEAD_DIM), 1.0),
        'cache_v': nrm(ks[3], (DEC_BATCH, DEPTH, PAST_LEN, NA_HEADS, NA_HEAD_DIM), 1.0),
        'c': nrm(ks[4], (DEC_BATCH, D_MODEL), 1.0),
        'c_ctx': nrm(ks[5], (D_MODEL,), 1.0),
        'w_mod': nrm(ks[6], (L, D_MODEL, N_MOD), D_MODEL ** -0.5),
        'b_mod': nrm(ks[7], (L, N_MOD), 0.01),
        'g_pre1': 1.0 + nrm(ks[8], (L, D_MODEL), 0.01),
        'g_post1': 1.0 + nrm(ks[9], (L, D_MODEL), 0.01),
        'g_pre2': 1.0 + nrm(ks[10], (L, D_MODEL), 0.01),
        'g_post2': 1.0 + nrm(ks[11], (L, D_MODEL), 0.01),
        'w_in': nrm(ks[12], (L, D_MODEL, N_IN), D_MODEL ** -0.5),
        'rpb': nrm(ks[13], (L, NA_HEADS, 2 * NA_ROWS_MAX - 1, 2 * NA_COLS - 1), 0.1),
        'w_na_out': nrm(ks[14], (L, NA_WIDTH, D_MODEL), NA_WIDTH ** -0.5),
        'w_fnet_out': nrm(ks[15], (L, FNET_WIDTH, D_MODEL), FNET_WIDTH ** -0.5),
        'conv_w': nrm(ks[16], (L, CONV_K, CONV_WIDTH), CONV_K ** -0.5),
        'conv_b': nrm(ks[17], (L, CONV_WIDTH), 0.01),
        'w_conv_out': nrm(ks[18], (L, CONV_WIDTH, D_MODEL), CONV_WIDTH ** -0.5),
        'w_o': nrm(ks[19], (L, D_MODEL, D_MODEL), D_MODEL ** -0.5),
        'w_up': nrm(ks[20], (L, D_MODEL, 2 * D_FF), D_MODEL ** -0.5),
        'ffn_conv_w': nrm(ks[21], (L, CONV_K, D_FF), CONV_K ** -0.5),
        'ffn_conv_b': nrm(ks[22], (L, D_FF), 0.01),
        'w_down': nrm(ks[23], (L, D_FF, D_MODEL), D_FF ** -0.5),
    }


def reference(x_prompt, x_sample, cache_k, cache_v, c, c_ctx, w_mod, b_mod, g_pre1, g_post1,
              g_pre2, g_post2, w_in, rpb, w_na_out, w_fnet_out, conv_w, conv_b, w_conv_out, w_o,
              w_up, ffn_conv_w, ffn_conv_b, w_down):
    xp = x_prompt
    xs = x_sample
    new_ks = []
    new_vs = []
    for l in range(DEPTH):
        p = {
            'g_pre1': g_pre1[l], 'g_post1': g_post1[l], 'g_pre2': g_pre2[l], 'g_post2': g_post2[l],
            'w_in': w_in[l], 'w_na_out': w_na_out[l], 'w_fnet_out': w_fnet_out[l],
            'conv_w': conv_w[l], 'conv_b': conv_b[l], 'w_conv_out': w_conv_out[l], 'w_o': w_o[l],
            'w_up': w_up[l], 'ffn_conv_w': ffn_conv_w[l], 'ffn_conv_b': ffn_conv_b[l],
            'w_down': w_down[l],
        }
        mod_ctx = (jax.nn.silu(c_ctx) @ w_mod[l] + b_mod[l])[None, :]
        xp, k_l, v_l = layer(xp, mod_ctx, p, ctx_attention)
        new_ks.append(k_l)
        new_vs.append(v_l)
        mod_lat = jax.nn.silu(c) @ w_mod[l] + b_mod[l]
        lat_attend = functools.partial(na_attention, k_ctx=cache_k[:, l], v_ctx=cache_v[:, l],
                                       rpb=rpb[l])
        xs, _, _ = layer(xs, mod_lat, p, lat_attend)
    new_k = jnp.stack(new_ks, axis=1)
    new_v = jnp.stack(new_vs, axis=1)
    return (xp, xs, new_k, new_v)
```

```python
import functools

import numpy as np
import jax
import jax.numpy as jnp
from jax import lax
from jax.experimental import pallas as pl
from jax.experimental.pallas import tpu as pltpu

F32 = jnp.float32
BF16 = jnp.bfloat16

EPS = 1e-6
NEG_INF = -1e30
GRID_W = 64
NA_ROWS = 8
NA_COLS = 16
N_HEADS = 16
HEAD_DIM = 128

VMEM_LIMIT_BYTES = 58 * 1024 * 1024
ROW_GROUP = 1024


def _params(*sem):
    return pltpu.CompilerParams(dimension_semantics=sem, vmem_limit_bytes=VMEM_LIMIT_BYTES)


def _bdot(a, b):
    return jnp.dot(a, b, preferred_element_type=F32)


def _mod_kernel(c_ref, w_ref, b_ref, o_ref):
    c = c_ref[...]
    s = (c * jax.nn.sigmoid(c)).astype(BF16)
    o_ref[...] = _bdot(s, w_ref[...].astype(BF16)) + b_ref[...]


def _modulation(c_rows, w_mod, b_mod, tn=512):
    L, D, N = w_mod.shape
    return pl.pallas_call(
        _mod_kernel,
        grid=(L, N // tn),
        in_specs=[
            pl.BlockSpec((8, D), lambda l, n: (0, 0)),
            pl.BlockSpec((None, D, tn), lambda l, n: (l, 0, n)),
            pl.BlockSpec((None, 1, tn), lambda l, n: (l, 0, n)),
        ],
        out_specs=pl.BlockSpec((None, 8, tn), lambda l, n: (l, 0, n)),
        out_shape=jax.ShapeDtypeStruct((L, 8, N), F32),
        compiler_params=_params("arbitrary", "arbitrary"),
        name="modulation",
    )(c_rows, w_mod, b_mod.reshape(L, 1, N))


def _rms(x):
    return x * lax.rsqrt(jnp.mean(x * x, axis=-1, keepdims=True) + EPS)


def _prenorm_kernel(x_ref, g_ref, sc_ref, sh_ref, h_ref):
    h = _rms(x_ref[...]) * g_ref[...]
    h_ref[...] = (h * (1.0 + sc_ref[...]) + sh_ref[...]).astype(BF16)


def _resid_kernel(x_ref, y_ref, gpost_ref, gate_ref, gpre_ref, sc_ref, sh_ref, xo_ref, h_ref):
    x = x_ref[...] + gate_ref[...] * (_rms(y_ref[...]) * gpost_ref[...])
    xo_ref[...] = x
    h = _rms(x) * gpre_ref[...]
    h_ref[...] = (h * (1.0 + sc_ref[...]) + sh_ref[...]).astype(BF16)


def _resid_last_kernel(x_ref, y_ref, gpost_ref, gate_ref, xo_ref):
    xo_ref[...] = x_ref[...] + gate_ref[...] * (_rms(y_ref[...]) * gpost_ref[...])


def _row_spec(tr, D):
    return pl.BlockSpec((tr, D), lambda i: (i, 0))


def _gain_spec(l, D):
    return pl.BlockSpec((None, 1, D), lambda i: (l, 0, 0))


def _mod_spec(l, j, tr, D):
    return pl.BlockSpec((None, None, None, 1, D), lambda i: (l, j, (i * tr) // ROW_GROUP, 0, 0))


def _prenorm(x, gains, mods, l, j_sc, j_sh, tr=256):
    M, D = x.shape
    return pl.pallas_call(
        _prenorm_kernel,
        grid=(M // tr,),
        in_specs=[_row_spec(tr, D), _gain_spec(l, D), _mod_spec(l, j_sc, tr, D), _mod_spec(l, j_sh, tr, D)],
        out_specs=_row_spec(tr, D),
        out_shape=jax.ShapeDtypeStruct((M, D), BF16),
        compiler_params=_params("arbitrary"),
        name="prenorm",
    )(x, gains, mods, mods)


def _resid(x, y, gpost, mods, l, j_gate, nxt, tr=256):
    M, D = x.shape
    ins = [_row_spec(tr, D), _row_spec(tr, D), _gain_spec(l, D), _mod_spec(l, j_gate, tr, D)]
    args = [x, y, gpost, mods]
    if nxt is None:
        return pl.pallas_call(
            _resid_last_kernel,
            grid=(M // tr,),
            in_specs=ins,
            out_specs=_row_spec(tr, D),
            out_shape=jax.ShapeDtypeStruct((M, D), F32),
            compiler_params=_params("arbitrary"),
            name="resid_last",
        )(*args), None
    gpre, ln, j_sc, j_sh = nxt
    ins += [_gain_spec(ln, D), _mod_spec(ln, j_sc, tr, D), _mod_spec(ln, j_sh, tr, D)]
    args += [gpre, mods, mods]
    return pl.pallas_call(
        _resid_kernel,
        grid=(M // tr,),
        in_specs=ins,
        out_specs=[_row_spec(tr, D), _row_spec(tr, D)],
        out_shape=[jax.ShapeDtypeStruct((M, D), F32), jax.ShapeDtypeStruct((M, D), BF16)],
        compiler_params=_params("arbitrary"),
        name="resid",
    )(*args)


def _mm_kernel(x_ref, w_ref, o_ref):
    o_ref[...] = _bdot(x_ref[...], w_ref[...].astype(BF16)).astype(o_ref.dtype)


def _matmul(x, w, l, out_dtype, tm, tn, name):
    M, K = x.shape
    N = w.shape[2]
    return pl.pallas_call(
        _mm_kernel,
        grid=(M // tm, N // tn),
        in_specs=[
            pl.BlockSpec((tm, K), lambda m, n: (m, 0)),
            pl.BlockSpec((None, K, tn), lambda m, n: (l, 0, n)),
        ],
        out_specs=pl.BlockSpec((tm, tn), lambda m, n: (m, n)),
        out_shape=jax.ShapeDtypeStruct((M, N), out_dtype),
        compiler_params=_params("arbitrary", "arbitrary"),
        name=name,
    )(x, w)


def _seq_pos(tm, n_ctx_tiles, seq_ctx, seq_lat):
    seq = jnp.where(pl.program_id(0) < n_ctx_tiles, seq_ctx, seq_lat)
    row = lax.broadcasted_iota(jnp.int32, (tm, 1), 0)
    return row & (seq - 1), seq


def _dwconv3(u, pos, seq, w_ref, b_ref):
    tm = u.shape[0]
    prev = jnp.where(pos == 0, 0.0, pltpu.roll(u, 1, 0))
    nxt = jnp.where(pos == seq - 1, 0.0, pltpu.roll(u, tm - 1, 0))
    return prev * w_ref[0:1, :] + u * w_ref[1:2, :] + nxt * w_ref[2:3, :] + b_ref[...]


def _ffn_up_kernel(h_ref, wg_ref, wv_ref, cw_ref, cb_ref, o_ref, *, n_ctx_tiles, seq_ctx, seq_lat):
    h = h_ref[...]
    gate = _bdot(h, wg_ref[...].astype(BF16))
    val = _bdot(h, wv_ref[...].astype(BF16))
    pos, seq = _seq_pos(h.shape[0], n_ctx_tiles, seq_ctx, seq_lat)
    g = _dwconv3(gate, pos, seq, cw_ref, cb_ref)
    o_ref[...] = (g * jax.nn.sigmoid(g) * val).astype(o_ref.dtype)


def _ffn_up(h, w_up, conv_w, conv_b, l, n_ctx_rows, seq_ctx, seq_lat, tm=1024, tn=256):
    M, K = h.shape
    L, _, N2 = w_up.shape
    F = N2 // 2
    nb = F // tn
    kern = functools.partial(_ffn_up_kernel, n_ctx_tiles=n_ctx_rows // tm, seq_ctx=seq_ctx, seq_lat=seq_lat)
    return pl.pallas_call(
        kern,
        grid=(M // tm, nb),
        in_specs=[
            pl.BlockSpec((tm, K), lambda m, n: (m, 0)),
            pl.BlockSpec((None, K, tn), lambda m, n: (l, 0, n)),
            pl.BlockSpec((None, K, tn), lambda m, n: (l, 0, nb + n)),
            pl.BlockSpec((None, 3, tn), lambda m, n: (l, 0, n)),
            pl.BlockSpec((None, 1, tn), lambda m, n: (l, 0, n)),
        ],
        out_specs=pl.BlockSpec((tm, tn), lambda m, n: (m, n)),
        out_shape=jax.ShapeDtypeStruct((M, F), BF16),
        compiler_params=_params("arbitrary", "arbitrary"),
        name="ffn_up",
    )(h, w_up, w_up, conv_w, conv_b.reshape(L, 1, F))


def _sconv_kernel(zx_ref, zb_ref, zc_ref, cw_ref, cb_ref, o_ref, *, n_ctx_tiles, seq_ctx, seq_lat):
    u = zc_ref[...] * zx_ref[...]
    pos, seq = _seq_pos(u.shape[0], n_ctx_tiles, seq_ctx, seq_lat)
    o_ref[...] = (zb_ref[...] * _dwconv3(u, pos, seq, cw_ref, cb_ref)).astype(o_ref.dtype)


def _sconv(z, col0, width, conv_w, conv_b, l, n_ctx_rows, seq_ctx, seq_lat, tm=1024, tn=512):
    M = z.shape[0]
    L = conv_w.shape[0]
    kern = functools.partial(_sconv_kernel, n_ctx_tiles=n_ctx_rows // tm, seq_ctx=seq_ctx, seq_lat=seq_lat)
    cb0 = col0 // tn
    wb = width // tn
    return pl.pallas_call(
        kern,
        grid=(M // tm, wb),
        in_specs=[
            pl.BlockSpec((tm, tn), lambda m, n: (m, cb0 + n)),
            pl.BlockSpec((tm, tn), lambda m, n: (m, cb0 + wb + n)),
            pl.BlockSpec((tm, tn), lambda m, n: (m, cb0 + 2 * wb + n)),
            pl.BlockSpec((None, 3, tn), lambda m, n: (l, 0, n)),
            pl.BlockSpec((None, 1, tn), lambda m, n: (l, 0, n)),
        ],
        out_specs=pl.BlockSpec((tm, tn), lambda m, n: (m, n)),
        out_shape=jax.ShapeDtypeStruct((M, width), BF16),
        compiler_params=_params("arbitrary", "arbitrary"),
        name="sconv",
    )(z, z, z, conv_w, conv_b.reshape(L, 1, width))


def _merge_kernel(a_ref, f_ref, c_ref, wa_ref, wf_ref, wc_ref, ga_ref, gf_ref, gc_ref, o_ref):
    a = _bdot(a_ref[...], wa_ref[...].astype(BF16))
    f = _bdot(f_ref[...], wf_ref[...].astype(BF16))
    c = _bdot(c_ref[...], wc_ref[...].astype(BF16))
    o = jax.nn.sigmoid(ga_ref[...]) * a + jax.nn.sigmoid(gf_ref[...]) * f + jax.nn.sigmoid(gc_ref[...]) * c
    o_ref[...] = o.astype(o_ref.dtype)


def _merge(att, fr, cv, w_na_out, w_fnet_out, w_conv_out, z, gate_col0, l, tm=512, tn=512):
    M = att.shape[0]
    D = w_na_out.shape[2]
    g0 = gate_col0 // tn
    gb = D // tn

    def act(a):
        return pl.BlockSpec((tm, a.shape[1]), lambda m, n: (m, 0))

    def wgt(w):
        return pl.BlockSpec((None, w.shape[1], tn), lambda m, n: (l, 0, n))

    def gate(j):
        return pl.BlockSpec((tm, tn), lambda m, n: (m, g0 + j * gb + n))

    return pl.pallas_call(
        _merge_kernel,
        grid=(M // tm, D // tn),
        in_specs=[act(att), act(fr), act(cv), wgt(w_na_out), wgt(w_fnet_out), wgt(w_conv_out),
                  gate(0), gate(1), gate(2)],
        out_specs=pl.BlockSpec((tm, tn), lambda m, n: (m, n)),
        out_shape=jax.ShapeDtypeStruct((M, D), BF16),
        compiler_params=_params("arbitrary", "arbitrary"),
        name="merge",
    )(att, fr, cv, w_na_out, w_fnet_out, w_conv_out, z, z, z)


def _softmax_pv(s_list, v_list):
    m = functools.reduce(jnp.maximum, [jnp.max(s, axis=-1, keepdims=True) for s in s_list])
    e_list = [jnp.exp(s - m) for s in s_list]
    denom = functools.reduce(jnp.add, [jnp.sum(e, axis=-1, keepdims=True) for e in e_list])
    inv = 1.0 / denom
    outs = [_bdot((e * inv).astype(BF16), v) for e, v in zip(e_list, v_list)]
    return functools.reduce(jnp.add, outs)


def _qk(q, k):
    return lax.dot_general(q, k, (((1,), (1,)), ((), ())), preferred_element_type=F32)


def _ctx_attn_kernel(q_ref, k_ref, v_ref, o_ref, *, scale):
    for h in range(N_HEADS):
        sl = slice(h * HEAD_DIM, (h + 1) * HEAD_DIM)
        q = q_ref[:, sl].astype(BF16)
        k = k_ref[:, sl].astype(BF16)
        v = v_ref[:, sl].astype(BF16)
        s = _qk(q, k) * scale
        o_ref[:, sl] = _softmax_pv([s], [v]).astype(o_ref.dtype)


def _ctx_attention(z, n_seq, seq):
    W = N_HEADS * HEAD_DIM
    kern = functools.partial(_ctx_attn_kernel, scale=HEAD_DIM ** -0.5)
    return pl.pallas_call(
        kern,
        grid=(n_seq,),
        in_specs=[pl.BlockSpec((seq, W), lambda b, j=j: (b, j)) for j in range(3)],
        out_specs=pl.BlockSpec((seq, W), lambda b: (b, 0)),
        out_shape=jax.ShapeDtypeStruct((n_seq * seq, W), BF16),
        compiler_params=_params("arbitrary"),
        name="ctx_attention",
    )(z, z, z)


def _na_attn_kernel(q_ref, k_ref, v_ref, kc_ref, vc_ref, bias_ref, o_ref, *, scale):
    q = q_ref[...].astype(BF16)
    s_lat = _qk(q, k_ref[...].astype(BF16)) * scale + bias_ref[...]
    s_ctx = _qk(q, kc_ref[...].astype(BF16)) * scale
    o = _softmax_pv([s_lat, s_ctx], [v_ref[...].astype(BF16), vc_ref[...].astype(BF16)])
    o_ref[...] = o.astype(o_ref.dtype)


def _na_attention(z, row0, n_seq, seq, cache_k, cache_v, bias, l):
    W = N_HEADS * HEAD_DIM
    P = cache_k.shape[2]
    r0 = row0 // seq
    kern = functools.partial(_na_attn_kernel, scale=HEAD_DIM ** -0.5)

    def zspec(j):
        return pl.BlockSpec((seq, HEAD_DIM), lambda h, b: (r0 + b, j * N_HEADS + h))

    cspec = pl.BlockSpec((None, None, P, HEAD_DIM), lambda h, b: (b, l, 0, h))
    return pl.pallas_call(
        kern,
        grid=(N_HEADS, n_seq),
        in_specs=[zspec(0), zspec(1), zspec(2), cspec, cspec,
                  pl.BlockSpec((None, seq, seq), lambda h, b: (h, 0, 0))],
        out_specs=pl.BlockSpec((seq, HEAD_DIM), lambda h, b: (b, h)),
        out_shape=jax.ShapeDtypeStruct((n_seq * seq, W), BF16),
        compiler_params=_params("arbitrary", "arbitrary"),
        name="na_attention",
    )(z, z, z, cache_k, cache_v, bias)


def _na_bias(rpb, rows):
    kr = min(NA_ROWS, rows)
    r = jnp.arange(rows)
    rs = jnp.clip(r - kr // 2, 0, rows - kr)
    row_ok = (r[None, :] >= rs[:, None]) & (r[None, :] < rs[:, None] + kr)
    row_idx = jnp.clip(r[None, :] - r[:, None] + NA_ROWS - 1, 0, 2 * NA_ROWS - 2)
    col = jnp.arange(GRID_W)
    cs = jnp.clip(col - NA_COLS // 2, 0, GRID_W - NA_COLS)
    col_ok = (col[None, :] >= cs[:, None]) & (col[None, :] < cs[:, None] + NA_COLS)
    col_idx = jnp.clip(col[None, :] - col[:, None] + NA_COLS - 1, 0, 2 * NA_COLS - 2)
    b = rpb.astype(F32)[:, row_idx[:, None, :, None], col_idx[None, :, None, :]]
    ok = row_ok[:, None, :, None] & col_ok[None, :, None, :]
    T = rows * GRID_W
    return jnp.where(ok[None], b, NEG_INF).reshape(rpb.shape[0], T, T)


def _fnet_kernel(x_ref, ct_ref, st_ref, cc_ref, sc_ref, o_ref):
    hi = lax.Precision.HIGHEST
    x = x_ref[...]
    xc = jnp.dot(x, cc_ref[...], precision=hi, preferred_element_type=F32)
    xs = jnp.dot(x, sc_ref[...], precision=hi, preferred_element_type=F32)
    y = (jnp.dot(ct_ref[...], xc, precision=hi, preferred_element_type=F32)
         - jnp.dot(st_ref[...], xs, precision=hi, preferred_element_type=F32))
    o_ref[...] = y.astype(o_ref.dtype)


def _dft_mats(n):
    k = np.arange(n, dtype=np.int64)
    ang = 2.0 * np.pi * ((k[:, None] * k[None, :]) % n) / n
    s = 1.0 / np.sqrt(n)
    return jnp.asarray(np.cos(ang) * s, F32), jnp.asarray(np.sin(ang) * s, F32)


def _fnet(z, col0, groups, gdim, row0, n_seq, seq):
    ct, st = _dft_mats(seq)
    cc, sc = _dft_mats(gdim)
    r0 = row0 // seq
    c0 = col0 // gdim

    def const(n):
        return pl.BlockSpec((n, n), lambda b, g: (0, 0))

    return pl.pallas_call(
        _fnet_kernel,
        grid=(n_seq, groups),
        in_specs=[pl.BlockSpec((seq, gdim), lambda b, g: (r0 + b, c0 + g)),
                  const(seq), const(seq), const(gdim), const(gdim)],
        out_specs=pl.BlockSpec((seq, gdim), lambda b, g: (b, g)),
        out_shape=jax.ShapeDtypeStruct((n_seq * seq, groups * gdim), BF16),
        compiler_params=_params("arbitrary", "arbitrary"),
        name="fnet",
    )(z, ct, st, cc, sc)


def kernel(x_prompt, x_sample, cache_k, cache_v, c, c_ctx, w_mod, b_mod, g_pre1, g_post1, g_pre2, g_post2,
           w_in, rpb, w_na_out, w_fnet_out, conv_w, conv_b, w_conv_out, w_o, w_up, ffn_conv_w, ffn_conv_b,
           w_down):
    B, S, D = x_prompt.shape
    Bd, T, _ = x_sample.shape
    L = w_mod.shape[0]
    P = cache_k.shape[2]
    W = N_HEADS * HEAD_DIM
    n_ctx = B * S
    n_lat = Bd * T
    fw = w_fnet_out.shape[1]
    cwid = w_conv_out.shape[1]
    fgroups = 4
    assert T == ROW_GROUP and n_ctx % ROW_GROUP == 0 and Bd + 1 <= 8

    x = jnp.concatenate([x_prompt.reshape(n_ctx, D), x_sample.reshape(n_lat, D)], axis=0)

    c_rows = jnp.zeros((8, D), F32).at[:Bd].set(c).at[Bd].set(c_ctx)
    mod = _modulation(c_rows, w_mod, b_mod)
    group_row = np.concatenate([np.full(n_ctx // ROW_GROUP, Bd), np.arange(Bd)])
    mods = mod.reshape(L, 8, 6, D)[:, group_row]
    mods = mods.transpose(0, 2, 1, 3)[:, :, :, None, :]

    gains = [g.reshape(L, 1, D) for g in (g_pre1, g_post1, g_pre2, g_post2)]
    ck = cache_k.reshape(Bd, L, P, W)
    cv_cache = cache_v.reshape(Bd, L, P, W)

    col_f = 3 * W
    col_conv = col_f + fw
    col_gate = col_conv + 3 * cwid

    new_k, new_v = [], []
    h = _prenorm(x, gains[0], mods, 0, 1, 0)
    for l in range(L):
        z = _matmul(h, w_in, l, F32, 1024, 512, "proj_in")
        new_k.append(z[:n_ctx, W:2 * W].reshape(B, S, N_HEADS, HEAD_DIM))
        new_v.append(z[:n_ctx, 2 * W:3 * W].reshape(B, S, N_HEADS, HEAD_DIM))
        bias = _na_bias(rpb[l], T // GRID_W)
        att = jnp.concatenate([
            _ctx_attention(z, B, S),
            _na_attention(z, n_ctx, Bd, T, ck, cv_cache, bias, l)], axis=0)
        fr = jnp.concatenate([
            _fnet(z, col_f, fgroups, fw // fgroups, 0, B, S),
            _fnet(z, col_f, fgroups, fw // fgroups, n_ctx, Bd, T)], axis=0)
        cv = _sconv(z, col_conv, cwid, conv_w, conv_b, l, n_ctx, S, T)
        merged = _merge(att, fr, cv, w_na_out, w_fnet_out, w_conv_out, z, col_gate, l)
        y = _matmul(merged, w_o, l, F32, 1024, 512, "proj_o")
        x, h = _resid(x, y, gains[1], mods, l, 2, (gains[2], l, 4, 3))
        act = _ffn_up(h, w_up, ffn_conv_w, ffn_conv_b, l, n_ctx, S, T)
        y = _matmul(act, w_down, l, F32, 512, 256, "ffn_down")
        nxt = (gains[0], l + 1, 1, 0) if l + 1 < L else None
        x, h = _resid(x, y, gains[3], mods, l, 5, nxt)

    y_prompt = x[:n_ctx].reshape(B, S, D)
    y_sample = x[n_ctx:].reshape(Bd, T, D)
    return y_prompt, y_sample, jnp.stack(new_k, axis=1), jnp.stack(new_v, axis=1)
```

```python
import functools

import numpy as np
import jax
import jax.numpy as jnp
from jax import lax
from jax.experimental import pallas as pl
from jax.experimental.pallas import tpu as pltpu

F32 = jnp.float32
BF16 = jnp.bfloat16

EPS = 1e-6
NEG_INF = -1e30
GRID_W = 64
NA_ROWS = 8
NA_COLS = 16
N_HEADS = 16
HEAD_DIM = 128

VMEM_LIMIT_BYTES = 58 * 1024 * 1024
ROW_GROUP = 1024


def _params(*sem):
    return pltpu.CompilerParams(dimension_semantics=sem, vmem_limit_bytes=VMEM_LIMIT_BYTES)


def _bdot(a, b):
    return jnp.dot(a, b, preferred_element_type=F32)


def _mod_kernel(c_ref, w_ref, b_ref, o_ref):
    c = c_ref[...]
    s = (c * jax.nn.sigmoid(c)).astype(BF16)
    o_ref[...] = _bdot(s, w_ref[...].astype(BF16)) + b_ref[...]


def _modulation(c_rows, w_mod, b_mod, tn=512):
    L, D, N = w_mod.shape
    return pl.pallas_call(
        _mod_kernel,
        grid=(L, N // tn),
        in_specs=[
            pl.BlockSpec((8, D), lambda l, n: (0, 0)),
            pl.BlockSpec((None, D, tn), lambda l, n: (l, 0, n)),
            pl.BlockSpec((None, 1, tn), lambda l, n: (l, 0, n)),
        ],
        out_specs=pl.BlockSpec((None, 8, tn), lambda l, n: (l, 0, n)),
        out_shape=jax.ShapeDtypeStruct((L, 8, N), F32),
        compiler_params=_params("arbitrary", "arbitrary"),
        name="modulation",
    )(c_rows, w_mod, b_mod.reshape(L, 1, N))


def _rms(x):
    return x * lax.rsqrt(jnp.mean(x * x, axis=-1, keepdims=True) + EPS)


def _prenorm_kernel(x_ref, g_ref, sc_ref, sh_ref, h_ref):
    h = _rms(x_ref[...]) * g_ref[...]
    h_ref[...] = (h * (1.0 + sc_ref[...]) + sh_ref[...]).astype(BF16)


def _resid_kernel(x_ref, y_ref, gpost_ref, gate_ref, gpre_ref, sc_ref, sh_ref, xo_ref, h_ref):
    x = x_ref[...] + gate_ref[...] * (_rms(y_ref[...]) * gpost_ref[...])
    xo_ref[...] = x
    h = _rms(x) * gpre_ref[...]
    h_ref[...] = (h * (1.0 + sc_ref[...]) + sh_ref[...]).astype(BF16)


def _resid_last_kernel(x_ref, y_ref, gpost_ref, gate_ref, xo_ref):
    xo_ref[...] = x_ref[...] + gate_ref[...] * (_rms(y_ref[...]) * gpost_ref[...])


def _row_spec(tr, D):
    return pl.BlockSpec((tr, D), lambda i: (i, 0))


def _gain_spec(l, D):
    return pl.BlockSpec((None, 1, D), lambda i: (l, 0, 0))


def _mod_spec(l, j, tr, D):
    return pl.BlockSpec((None, None, None, 1, D), lambda i: (l, j, (i * tr) // ROW_GROUP, 0, 0))


def _prenorm(x, gains, mods, l, j_sc, j_sh, tr=256):
    M, D = x.shape
    return pl.pallas_call(
        _prenorm_kernel,
        grid=(M // tr,),
        in_specs=[_row_spec(tr, D), _gain_spec(l, D), _mod_spec(l, j_sc, tr, D), _mod_spec(l, j_sh, tr, D)],
        out_specs=_row_spec(tr, D),
        out_shape=jax.ShapeDtypeStruct((M, D), BF16),
        compiler_params=_params("arbitrary"),
        name="prenorm",
    )(x, gains, mods, mods)


def _resid(x, y, gpost, mods, l, j_gate, nxt, tr=256):
    M, D = x.shape
    ins = [_row_spec(tr, D), _row_spec(tr, D), _gain_spec(l, D), _mod_spec(l, j_gate, tr, D)]
    args = [x, y, gpost, mods]
    if nxt is None:
        return pl.pallas_call(
            _resid_last_kernel,
            grid=(M // tr,),
            in_specs=ins,
            out_specs=_row_spec(tr, D),
            out_shape=jax.ShapeDtypeStruct((M, D), F32),
            compiler_params=_params("arbitrary"),
            name="resid_last",
        )(*args), None
    gpre, ln, j_sc, j_sh = nxt
    ins += [_gain_spec(ln, D), _mod_spec(ln, j_sc, tr, D), _mod_spec(ln, j_sh, tr, D)]
    args += [gpre, mods, mods]
    return pl.pallas_call(
        _resid_kernel,
        grid=(M // tr,),
        in_specs=ins,
        out_specs=[_row_spec(tr, D), _row_spec(tr, D)],
        out_shape=[jax.ShapeDtypeStruct((M, D), F32), jax.ShapeDtypeStruct((M, D), BF16)],
        compiler_params=_params("arbitrary"),
        name="resid",
    )(*args)


def _mm_kernel(x_ref, w_ref, o_ref):
    o_ref[...] = _bdot(x_ref[...], w_ref[...].astype(BF16)).astype(o_ref.dtype)


def _act_spec(tm, K, single_buffer):
    if single_buffer:
        return pl.BlockSpec((tm, K), lambda m, n: (m, 0), pipeline_mode=pl.Buffered(1))
    return pl.BlockSpec((tm, K), lambda m, n: (m, 0))


def _matmul(x, w, l, out_dtype, tm, tn, name, col0=0, ncols=None, single_buffer=False):
    M, K = x.shape
    N = w.shape[2] - col0 if ncols is None else ncols
    c0 = col0 // tn
    return pl.pallas_call(
        _mm_kernel,
        grid=(M // tm, N // tn),
        in_specs=[
            _act_spec(tm, K, single_buffer),
            pl.BlockSpec((None, K, tn), lambda m, n: (l, 0, c0 + n)),
        ],
        out_specs=pl.BlockSpec((tm, tn), lambda m, n: (m, n)),
        out_shape=jax.ShapeDtypeStruct((M, N), out_dtype),
        compiler_params=_params("arbitrary", "arbitrary"),
        name=name,
    )(x, w)


def _seq_pos(tm, n_ctx_tiles, seq_ctx, seq_lat):
    seq = jnp.where(pl.program_id(0) < n_ctx_tiles, seq_ctx, seq_lat)
    row = lax.broadcasted_iota(jnp.int32, (tm, 1), 0)
    return row & (seq - 1), seq


def _dwconv3(u, pos, seq, w_ref, b_ref):
    tm = u.shape[0]
    prev = jnp.where(pos == 0, 0.0, pltpu.roll(u, 1, 0))
    nxt = jnp.where(pos == seq - 1, 0.0, pltpu.roll(u, tm - 1, 0))
    return prev * w_ref[0:1, :] + u * w_ref[1:2, :] + nxt * w_ref[2:3, :] + b_ref[...]


def _ffn_up_kernel(h_ref, wg_ref, wv_ref, cw_ref, cb_ref, o_ref, *, n_ctx_tiles, seq_ctx, seq_lat):
    h = h_ref[...]
    gate = _bdot(h, wg_ref[...].astype(BF16))
    val = _bdot(h, wv_ref[...].astype(BF16))
    pos, seq = _seq_pos(h.shape[0], n_ctx_tiles, seq_ctx, seq_lat)
    g = _dwconv3(gate, pos, seq, cw_ref, cb_ref)
    o_ref[...] = (g * jax.nn.sigmoid(g) * val).astype(o_ref.dtype)


def _ffn_up(h, w_up, conv_w, conv_b, l, n_ctx_rows, seq_ctx, seq_lat, tm=1024, tn=256):
    M, K = h.shape
    L, _, N2 = w_up.shape
    F = N2 // 2
    nb = F // tn
    kern = functools.partial(_ffn_up_kernel, n_ctx_tiles=n_ctx_rows // tm, seq_ctx=seq_ctx, seq_lat=seq_lat)
    return pl.pallas_call(
        kern,
        grid=(M // tm, nb),
        in_specs=[
            pl.BlockSpec((tm, K), lambda m, n: (m, 0)),
            pl.BlockSpec((None, K, tn), lambda m, n: (l, 0, n)),
            pl.BlockSpec((None, K, tn), lambda m, n: (l, 0, nb + n)),
            pl.BlockSpec((None, 3, tn), lambda m, n: (l, 0, n)),
            pl.BlockSpec((None, 1, tn), lambda m, n: (l, 0, n)),
        ],
        out_specs=pl.BlockSpec((tm, tn), lambda m, n: (m, n)),
        out_shape=jax.ShapeDtypeStruct((M, F), BF16),
        compiler_params=_params("arbitrary", "arbitrary"),
        name="ffn_up",
    )(h, w_up, w_up, conv_w, conv_b.reshape(L, 1, F))


def _sconv_kernel(zx_ref, zb_ref, zc_ref, cw_ref, cb_ref, o_ref, *, n_ctx_tiles, seq_ctx, seq_lat):
    u = zc_ref[...] * zx_ref[...]
    pos, seq = _seq_pos(u.shape[0], n_ctx_tiles, seq_ctx, seq_lat)
    o_ref[...] = (zb_ref[...] * _dwconv3(u, pos, seq, cw_ref, cb_ref)).astype(o_ref.dtype)


def _sconv(z, col0, width, conv_w, conv_b, l, n_ctx_rows, seq_ctx, seq_lat, tm=1024, tn=512):
    M = z.shape[0]
    L = conv_w.shape[0]
    kern = functools.partial(_sconv_kernel, n_ctx_tiles=n_ctx_rows // tm, seq_ctx=seq_ctx, seq_lat=seq_lat)
    cb0 = col0 // tn
    wb = width // tn
    return pl.pallas_call(
        kern,
        grid=(M // tm, wb),
        in_specs=[
            pl.BlockSpec((tm, tn), lambda m, n: (m, cb0 + n)),
            pl.BlockSpec((tm, tn), lambda m, n: (m, cb0 + wb + n)),
            pl.BlockSpec((tm, tn), lambda m, n: (m, cb0 + 2 * wb + n)),
            pl.BlockSpec((None, 3, tn), lambda m, n: (l, 0, n)),
            pl.BlockSpec((None, 1, tn), lambda m, n: (l, 0, n)),
        ],
        out_specs=pl.BlockSpec((tm, tn), lambda m, n: (m, n)),
        out_shape=jax.ShapeDtypeStruct((M, width), BF16),
        compiler_params=_params("arbitrary", "arbitrary"),
        name="sconv",
    )(z, z, z, conv_w, conv_b.reshape(L, 1, width))


def _merge_kernel(a_ref, f_ref, c_ref, wa_ref, wf_ref, wc_ref, ga_ref, gf_ref, gc_ref, o_ref):
    a = _bdot(a_ref[...], wa_ref[...].astype(BF16))
    f = _bdot(f_ref[...], wf_ref[...].astype(BF16))
    c = _bdot(c_ref[...], wc_ref[...].astype(BF16))
    def sig(g_ref):
        return jax.nn.sigmoid(g_ref[...].astype(F32))

    o = sig(ga_ref) * a + sig(gf_ref) * f + sig(gc_ref) * c
    o_ref[...] = o.astype(o_ref.dtype)


def _merge(att, fr, cv, w_na_out, w_fnet_out, w_conv_out, z, l, tm=1024, tn=512):
    M = att.shape[0]
    D = w_na_out.shape[2]
    gb = D // tn

    def act(a):
        return _act_spec(tm, a.shape[1], True)

    def wgt(w):
        return pl.BlockSpec((None, w.shape[1], tn), lambda m, n: (l, 0, n))

    def gate(j):
        return pl.BlockSpec((tm, tn), lambda m, n: (m, j * gb + n))

    return pl.pallas_call(
        _merge_kernel,
        grid=(M // tm, D // tn),
        in_specs=[act(att), act(fr), act(cv), wgt(w_na_out), wgt(w_fnet_out), wgt(w_conv_out),
                  gate(0), gate(1), gate(2)],
        out_specs=pl.BlockSpec((tm, tn), lambda m, n: (m, n)),
        out_shape=jax.ShapeDtypeStruct((M, D), BF16),
        compiler_params=_params("arbitrary", "arbitrary"),
        name="merge",
    )(att, fr, cv, w_na_out, w_fnet_out, w_conv_out, z, z, z)


def _softmax_pv(s_list, v_list):
    m = functools.reduce(jnp.maximum, [jnp.max(s, axis=-1, keepdims=True) for s in s_list])
    e_list = [jnp.exp(s - m) for s in s_list]
    denom = functools.reduce(jnp.add, [jnp.sum(e, axis=-1, keepdims=True) for e in e_list])
    inv = 1.0 / denom
    outs = [_bdot((e * inv).astype(BF16), v) for e, v in zip(e_list, v_list)]
    return functools.reduce(jnp.add, outs)


def _qk(q, k):
    return lax.dot_general(q, k, (((1,), (1,)), ((), ())), preferred_element_type=F32)


def _ctx_attn_kernel(q_ref, k_ref, v_ref, o_ref, *, scale):
    for h in range(N_HEADS):
        sl = slice(h * HEAD_DIM, (h + 1) * HEAD_DIM)
        q = q_ref[:, sl].astype(BF16)
        k = k_ref[:, sl].astype(BF16)
        v = v_ref[:, sl].astype(BF16)
        s = _qk(q, k) * scale
        o_ref[:, sl] = _softmax_pv([s], [v]).astype(o_ref.dtype)


def _ctx_attention(z, n_seq, seq):
    W = N_HEADS * HEAD_DIM
    kern = functools.partial(_ctx_attn_kernel, scale=HEAD_DIM ** -0.5)
    return pl.pallas_call(
        kern,
        grid=(n_seq,),
        in_specs=[pl.BlockSpec((seq, W), lambda b, j=j: (b, j)) for j in range(3)],
        out_specs=pl.BlockSpec((seq, W), lambda b: (b, 0)),
        out_shape=jax.ShapeDtypeStruct((n_seq * seq, W), BF16),
        compiler_params=_params("arbitrary"),
        name="ctx_attention",
    )(z, z, z)


def _window_starts(rows):
    kr = min(NA_ROWS, rows)
    return [min(max(r - kr // 2, 0), rows - kr) for r in range(rows)], kr


def _row_groups(rows):
    starts, kr = _window_starts(rows)
    groups = []
    for r, rs in enumerate(starts):
        if groups and groups[-1][2] == rs:
            groups[-1][1] = r + 1
        else:
            groups.append([r, r + 1, rs])
    return groups, kr


def _na_attn_kernel(q_ref, k_ref, v_ref, kc_ref, vc_ref, bias_ref, o_ref, *, scale, rows):
    groups, kr = _row_groups(rows)
    q = q_ref[...].astype(BF16)
    k = k_ref[...].astype(BF16)
    v = v_ref[...].astype(BF16)
    vc = vc_ref[...].astype(BF16)
    s_ctx = _qk(q, kc_ref[...].astype(BF16)) * scale
    for r0, r1, rs in groups:
        qs = slice(r0 * GRID_W, r1 * GRID_W)
        ks = slice(rs * GRID_W, (rs + kr) * GRID_W)
        bias = bias_ref[r0:r1].reshape((r1 - r0) * GRID_W, kr * GRID_W)
        s_lat = _qk(q[qs], k[ks]) * scale + bias
        o_ref[qs, :] = _softmax_pv([s_lat, s_ctx[qs]], [v[ks], vc]).astype(o_ref.dtype)


def _na_attention(qkv, row0, n_seq, seq, cache_k, cache_v, bias, l):
    W = N_HEADS * HEAD_DIM
    P = cache_k.shape[2]
    r0 = row0 // seq
    rows = seq // GRID_W
    kern = functools.partial(_na_attn_kernel, scale=HEAD_DIM ** -0.5, rows=rows)

    def zspec(j):
        return pl.BlockSpec((seq, HEAD_DIM), lambda h, b: (r0 + b, j * N_HEADS + h))

    cspec = pl.BlockSpec((None, None, P, HEAD_DIM), lambda h, b: (b, l, 0, h))
    return pl.pallas_call(
        kern,
        grid=(N_HEADS, n_seq),
        in_specs=[zspec(0), zspec(1), zspec(2), cspec, cspec,
                  pl.BlockSpec((None, None) + bias.shape[2:], lambda h, b: (l, h, 0, 0, 0))],
        out_specs=pl.BlockSpec((seq, HEAD_DIM), lambda h, b: (b, h)),
        out_shape=jax.ShapeDtypeStruct((n_seq * seq, W), BF16),
        compiler_params=_params("arbitrary", "arbitrary"),
        name="na_attention",
    )(qkv, qkv, qkv, cache_k, cache_v, bias)


def _na_bias(rpb, rows):
    starts, kr = _window_starts(rows)
    col = np.arange(GRID_W)
    cs = np.clip(col - NA_COLS // 2, 0, GRID_W - NA_COLS)
    col_ok = (col[None, :] >= cs[:, None]) & (col[None, :] < cs[:, None] + NA_COLS)
    col_idx = np.clip(col[None, :] - col[:, None] + NA_COLS - 1, 0, 2 * NA_COLS - 2)
    onehot = (col_idx[None] == np.arange(2 * NA_COLS - 1)[:, None, None]) & col_ok[None]
    e = jnp.einsum('lhrd,dqk->lhrqk', rpb.astype(F32), jnp.asarray(onehot, F32),
                   precision=lax.Precision.HIGHEST)
    e = jnp.where(jnp.asarray(col_ok), e, NEG_INF)
    per_row = []
    for r, rs in enumerate(starts):
        d0 = rs - r + NA_ROWS - 1
        band = e[:, :, d0:d0 + kr].transpose(0, 1, 3, 2, 4)
        per_row.append(band.reshape(band.shape[:3] + (kr * GRID_W,)))
    return jnp.stack(per_row, axis=2)


def _fnet_kernel(x_ref, ct_ref, st_ref, cc_ref, sc_ref, o_ref):
    hi = lax.Precision.HIGHEST
    x = x_ref[...]
    xc = jnp.dot(x, cc_ref[...], precision=hi, preferred_element_type=F32)
    xs = jnp.dot(x, sc_ref[...], precision=hi, preferred_element_type=F32)
    y = (jnp.dot(ct_ref[...], xc, precision=hi, preferred_element_type=F32)
         - jnp.dot(st_ref[...], xs, precision=hi, preferred_element_type=F32))
    o_ref[...] = y.astype(o_ref.dtype)


def _dft_mats(n):
    k = np.arange(n, dtype=np.int64)
    ang = 2.0 * np.pi * ((k[:, None] * k[None, :]) % n) / n
    s = 1.0 / np.sqrt(n)
    return jnp.asarray(np.cos(ang) * s, F32), jnp.asarray(np.sin(ang) * s, F32)


def _fnet(z, col0, groups, gdim, row0, n_seq, seq):
    ct, st = _dft_mats(seq)
    cc, sc = _dft_mats(gdim)
    r0 = row0 // seq
    c0 = col0 // gdim

    def const(n):
        return pl.BlockSpec((n, n), lambda b, g: (0, 0))

    return pl.pallas_call(
        _fnet_kernel,
        grid=(n_seq, groups),
        in_specs=[pl.BlockSpec((seq, gdim), lambda b, g: (r0 + b, c0 + g)),
                  const(seq), const(seq), const(gdim), const(gdim)],
        out_specs=pl.BlockSpec((seq, gdim), lambda b, g: (b, g)),
        out_shape=jax.ShapeDtypeStruct((n_seq * seq, groups * gdim), BF16),
        compiler_params=_params("arbitrary", "arbitrary"),
        name="fnet",
    )(z, ct, st, cc, sc)


def kernel(x_prompt, x_sample, cache_k, cache_v, c, c_ctx, w_mod, b_mod, g_pre1, g_post1, g_pre2, g_post2,
           w_in, rpb, w_na_out, w_fnet_out, conv_w, conv_b, w_conv_out, w_o, w_up, ffn_conv_w, ffn_conv_b,
           w_down):
    B, S, D = x_prompt.shape
    Bd, T, _ = x_sample.shape
    L = w_mod.shape[0]
    P = cache_k.shape[2]
    W = N_HEADS * HEAD_DIM
    n_ctx = B * S
    n_lat = Bd * T
    fw = w_fnet_out.shape[1]
    cwid = w_conv_out.shape[1]
    fgroups = 4
    assert T == ROW_GROUP and n_ctx % ROW_GROUP == 0 and Bd + 1 <= 8

    x = jnp.concatenate([x_prompt.reshape(n_ctx, D), x_sample.reshape(n_lat, D)], axis=0)

    c_rows = jnp.zeros((8, D), F32).at[:Bd].set(c).at[Bd].set(c_ctx)
    mod = _modulation(c_rows, w_mod, b_mod)
    group_row = np.concatenate([np.full(n_ctx // ROW_GROUP, Bd), np.arange(Bd)])
    mods = mod.reshape(L, 8, 6, D)[:, group_row]
    mods = mods.transpose(0, 2, 1, 3)[:, :, :, None, :]

    gains = [g.reshape(L, 1, D) for g in (g_pre1, g_post1, g_pre2, g_post2)]
    ck = cache_k.reshape(Bd, L, P, W)
    cv_cache = cache_v.reshape(Bd, L, P, W)

    col_f = 3 * W
    col_gate = col_f + fw + 3 * cwid
    bias = _na_bias(rpb, T // GRID_W)

    new_k, new_v = [], []
    h = _prenorm(x, gains[0], mods, 0, 1, 0)
    for l in range(L):
        qkv = _matmul(h, w_in, l, F32, 1024, 512, "proj_qkv", 0, col_f)
        fxbc = _matmul(h, w_in, l, F32, 1024, 512, "proj_fxbc", col_f, col_gate - col_f)
        zg = _matmul(h, w_in, l, BF16, 1024, 512, "proj_gates", col_gate)
        new_k.append(qkv[:n_ctx, W:2 * W].reshape(B, S, N_HEADS, HEAD_DIM))
        new_v.append(qkv[:n_ctx, 2 * W:3 * W].reshape(B, S, N_HEADS, HEAD_DIM))
        att = jnp.concatenate([
            _ctx_attention(qkv, B, S),
            _na_attention(qkv, n_ctx, Bd, T, ck, cv_cache, bias, l)], axis=0)
        fr = jnp.concatenate([
            _fnet(fxbc, 0, fgroups, fw // fgroups, 0, B, S),
            _fnet(fxbc, 0, fgroups, fw // fgroups, n_ctx, Bd, T)], axis=0)
        cv = _sconv(fxbc, fw, cwid, conv_w, conv_b, l, n_ctx, S, T)
        merged = _merge(att, fr, cv, w_na_out, w_fnet_out, w_conv_out, zg, l)
        y = _matmul(merged, w_o, l, F32, 1024, 512, "proj_o")
        x, h = _resid(x, y, gains[1], mods, l, 2, (gains[2], l, 4, 3))
        act = _ffn_up(h, w_up, ffn_conv_w, ffn_conv_b, l, n_ctx, S, T)
        y = _matmul(act, w_down, l, F32, 1024, 256, "ffn_down", single_buffer=True)
        nxt = (gains[0], l + 1, 1, 0) if l + 1 < L else None
        x, h = _resid(x, y, gains[3], mods, l, 5, nxt)

    y_prompt = x[:n_ctx].reshape(B, S, D)
    y_sample = x[n_ctx:].reshape(Bd, T, D)
    return y_prompt, y_sample, jnp.stack(new_k, axis=1), jnp.stack(new_v, axis=1)
```

```python
import functools

import numpy as np
import jax
import jax.numpy as jnp
from jax import lax
from jax.experimental import pallas as pl
from jax.experimental.pallas import tpu as pltpu

F32 = jnp.float32
BF16 = jnp.bfloat16

EPS = 1e-6
NEG_INF = -1e30
GRID_W = 64
NA_ROWS = 8
NA_COLS = 16
N_HEADS = 16
HEAD_DIM = 128
NA_GROUP_ROWS = 8

VMEM_LIMIT_BYTES = 58 * 1024 * 1024
ROW_GROUP = 1024


def _params(*sem):
    return pltpu.CompilerParams(dimension_semantics=sem, vmem_limit_bytes=VMEM_LIMIT_BYTES)


def _bdot(a, b):
    return jnp.dot(a, b, preferred_element_type=F32)


def _mod_kernel(c_ref, w_ref, b_ref, o_ref):
    c = c_ref[...]
    s = (c * jax.nn.sigmoid(c)).astype(BF16)
    o_ref[...] = _bdot(s, w_ref[...].astype(BF16)) + b_ref[...]


def _modulation(c_rows, w_mod, b_mod, tn=512):
    L, D, N = w_mod.shape
    return pl.pallas_call(
        _mod_kernel,
        grid=(L, N // tn),
        in_specs=[
            pl.BlockSpec((8, D), lambda l, n: (0, 0)),
            pl.BlockSpec((None, D, tn), lambda l, n: (l, 0, n)),
            pl.BlockSpec((None, 1, tn), lambda l, n: (l, 0, n)),
        ],
        out_specs=pl.BlockSpec((None, 8, tn), lambda l, n: (l, 0, n)),
        out_shape=jax.ShapeDtypeStruct((L, 8, N), F32),
        compiler_params=_params("arbitrary", "arbitrary"),
        name="modulation",
    )(c_rows, w_mod, b_mod.reshape(L, 1, N))


def _rms(x):
    return x * lax.rsqrt(jnp.mean(x * x, axis=-1, keepdims=True) + EPS)


def _prenorm_kernel(x_ref, g_ref, sc_ref, sh_ref, h_ref):
    h = _rms(x_ref[...]) * g_ref[...]
    h_ref[...] = (h * (1.0 + sc_ref[...]) + sh_ref[...]).astype(BF16)


def _resid_kernel(x_ref, y_ref, gpost_ref, gate_ref, gpre_ref, sc_ref, sh_ref, xo_ref, h_ref):
    x = x_ref[...] + gate_ref[...] * (_rms(y_ref[...]) * gpost_ref[...])
    xo_ref[...] = x
    h = _rms(x) * gpre_ref[...]
    h_ref[...] = (h * (1.0 + sc_ref[...]) + sh_ref[...]).astype(BF16)


def _resid_last_kernel(x_ref, y_ref, gpost_ref, gate_ref, xo_ref):
    xo_ref[...] = x_ref[...] + gate_ref[...] * (_rms(y_ref[...]) * gpost_ref[...])


def _row_spec(tr, D):
    return pl.BlockSpec((tr, D), lambda i: (i, 0))


def _gain_spec(l, D):
    return pl.BlockSpec((None, 1, D), lambda i: (l, 0, 0))


def _mod_spec(l, j, tr, D):
    return pl.BlockSpec((None, None, None, 1, D), lambda i: (l, j, (i * tr) // ROW_GROUP, 0, 0))


def _prenorm(x, gains, mods, l, j_sc, j_sh, tr=256):
    M, D = x.shape
    return pl.pallas_call(
        _prenorm_kernel,
        grid=(M // tr,),
        in_specs=[_row_spec(tr, D), _gain_spec(l, D), _mod_spec(l, j_sc, tr, D), _mod_spec(l, j_sh, tr, D)],
        out_specs=_row_spec(tr, D),
        out_shape=jax.ShapeDtypeStruct((M, D), BF16),
        compiler_params=_params("arbitrary"),
        name="prenorm",
    )(x, gains, mods, mods)


def _resid(x, y, gpost, mods, l, j_gate, nxt, tr=256):
    M, D = x.shape
    ins = [_row_spec(tr, D), _row_spec(tr, D), _gain_spec(l, D), _mod_spec(l, j_gate, tr, D)]
    args = [x, y, gpost, mods]
    if nxt is None:
        return pl.pallas_call(
            _resid_last_kernel,
            grid=(M // tr,),
            in_specs=ins,
            out_specs=_row_spec(tr, D),
            out_shape=jax.ShapeDtypeStruct((M, D), F32),
            compiler_params=_params("arbitrary"),
            name="resid_last",
        )(*args), None
    gpre, ln, j_sc, j_sh = nxt
    ins += [_gain_spec(ln, D), _mod_spec(ln, j_sc, tr, D), _mod_spec(ln, j_sh, tr, D)]
    args += [gpre, mods, mods]
    return pl.pallas_call(
        _resid_kernel,
        grid=(M // tr,),
        in_specs=ins,
        out_specs=[_row_spec(tr, D), _row_spec(tr, D)],
        out_shape=[jax.ShapeDtypeStruct((M, D), F32), jax.ShapeDtypeStruct((M, D), BF16)],
        compiler_params=_params("arbitrary"),
        name="resid",
    )(*args)


def _mm_kernel(x_ref, w_ref, o_ref):
    o_ref[...] = _bdot(x_ref[...], w_ref[...].astype(BF16)).astype(o_ref.dtype)


def _act_spec(tm, K, single_buffer, m0=0):
    if single_buffer:
        return pl.BlockSpec((tm, K), lambda m, n: (m0 + m, 0), pipeline_mode=pl.Buffered(1))
    return pl.BlockSpec((tm, K), lambda m, n: (m0 + m, 0))


def _matmul(x, w, l, out_dtype, tm, tn, name, col0=0, ncols=None, row0=0, nrows=None, single_buffer=False):
    K = x.shape[1]
    M = x.shape[0] - row0 if nrows is None else nrows
    N = w.shape[2] - col0 if ncols is None else ncols
    c0 = col0 // tn
    return pl.pallas_call(
        _mm_kernel,
        grid=(M // tm, N // tn),
        in_specs=[
            _act_spec(tm, K, single_buffer, row0 // tm),
            pl.BlockSpec((None, K, tn), lambda m, n: (l, 0, c0 + n)),
        ],
        out_specs=pl.BlockSpec((tm, tn), lambda m, n: (m, n)),
        out_shape=jax.ShapeDtypeStruct((M, N), out_dtype),
        compiler_params=_params("arbitrary", "arbitrary"),
        name=name,
    )(x, w)


def _cache_proj_kernel(x_ref, w_ref, *rest):
    o_ref = rest[-1]
    y = _bdot(x_ref[...], w_ref[...].astype(BF16))
    o_ref[...] = y.reshape(o_ref.shape)


def _cache_proj(h, w, l, col0, cache, n_seq, seq, depth, tm, tn, name):
    K = h.shape[1]
    W = N_HEADS * HEAD_DIM
    c0 = col0 // tn
    spt = tm // seq
    in_specs = [
        pl.BlockSpec((tm, K), lambda m, n: (m, 0)),
        pl.BlockSpec((None, K, tn), lambda m, n: (l, 0, c0 + n)),
    ]
    args = [h, w]
    aliases = {}
    if cache is not None:
        in_specs.append(pl.BlockSpec(memory_space=pl.ANY))
        args.append(cache)
        aliases = {2: 0}
    return pl.pallas_call(
        _cache_proj_kernel,
        grid=(n_seq * seq // tm, W // tn),
        in_specs=in_specs,
        out_specs=pl.BlockSpec((spt, None, seq, tn), lambda m, n: (m, l, 0, n)),
        out_shape=jax.ShapeDtypeStruct((n_seq, depth, seq, W), F32),
        input_output_aliases=aliases,
        compiler_params=_params("arbitrary", "arbitrary"),
        name=name,
    )(*args)


def _seq_pos(tm, n_ctx_tiles, seq_ctx, seq_lat):
    seq = jnp.where(pl.program_id(0) < n_ctx_tiles, seq_ctx, seq_lat)
    row = lax.broadcasted_iota(jnp.int32, (tm, 1), 0)
    return row & (seq - 1), seq


def _dwconv3(u, pos, seq, w_ref, b_ref):
    tm = u.shape[0]
    prev = jnp.where(pos == 0, 0.0, pltpu.roll(u, 1, 0))
    nxt = jnp.where(pos == seq - 1, 0.0, pltpu.roll(u, tm - 1, 0))
    return prev * w_ref[0:1, :] + u * w_ref[1:2, :] + nxt * w_ref[2:3, :] + b_ref[...]


def _ffn_up_kernel(h_ref, wg_ref, wv_ref, cw_ref, cb_ref, o_ref, *, n_ctx_tiles, seq_ctx, seq_lat):
    h = h_ref[...]
    gate = _bdot(h, wg_ref[...].astype(BF16))
    val = _bdot(h, wv_ref[...].astype(BF16))
    pos, seq = _seq_pos(h.shape[0], n_ctx_tiles, seq_ctx, seq_lat)
    g = _dwconv3(gate, pos, seq, cw_ref, cb_ref)
    o_ref[...] = (g * jax.nn.sigmoid(g) * val).astype(o_ref.dtype)


def _ffn_up(h, w_up, conv_w, conv_b, l, n_ctx_rows, seq_ctx, seq_lat, tm=1024, tn=256):
    M, K = h.shape
    L, _, N2 = w_up.shape
    F = N2 // 2
    nb = F // tn
    kern = functools.partial(_ffn_up_kernel, n_ctx_tiles=n_ctx_rows // tm, seq_ctx=seq_ctx, seq_lat=seq_lat)
    return pl.pallas_call(
        kern,
        grid=(M // tm, nb),
        in_specs=[
            pl.BlockSpec((tm, K), lambda m, n: (m, 0)),
            pl.BlockSpec((None, K, tn), lambda m, n: (l, 0, n)),
            pl.BlockSpec((None, K, tn), lambda m, n: (l, 0, nb + n)),
            pl.BlockSpec((None, 3, tn), lambda m, n: (l, 0, n)),
            pl.BlockSpec((None, 1, tn), lambda m, n: (l, 0, n)),
        ],
        out_specs=pl.BlockSpec((tm, tn), lambda m, n: (m, n)),
        out_shape=jax.ShapeDtypeStruct((M, F), BF16),
        compiler_params=_params("arbitrary", "arbitrary"),
        name="ffn_up",
    )(h, w_up, w_up, conv_w, conv_b.reshape(L, 1, F))


def _sconv_kernel(zx_ref, zb_ref, zc_ref, cw_ref, cb_ref, o_ref, *, n_ctx_tiles, seq_ctx, seq_lat):
    u = zc_ref[...] * zx_ref[...]
    pos, seq = _seq_pos(u.shape[0], n_ctx_tiles, seq_ctx, seq_lat)
    o_ref[...] = (zb_ref[...] * _dwconv3(u, pos, seq, cw_ref, cb_ref)).astype(o_ref.dtype)


def _sconv(z, col0, width, conv_w, conv_b, l, n_ctx_rows, seq_ctx, seq_lat, tm=1024, tn=512):
    M = z.shape[0]
    L = conv_w.shape[0]
    kern = functools.partial(_sconv_kernel, n_ctx_tiles=n_ctx_rows // tm, seq_ctx=seq_ctx, seq_lat=seq_lat)
    cb0 = col0 // tn
    wb = width // tn
    return pl.pallas_call(
        kern,
        grid=(M // tm, wb),
        in_specs=[
            pl.BlockSpec((tm, tn), lambda m, n: (m, cb0 + n)),
            pl.BlockSpec((tm, tn), lambda m, n: (m, cb0 + wb + n)),
            pl.BlockSpec((tm, tn), lambda m, n: (m, cb0 + 2 * wb + n)),
            pl.BlockSpec((None, 3, tn), lambda m, n: (l, 0, n)),
            pl.BlockSpec((None, 1, tn), lambda m, n: (l, 0, n)),
        ],
        out_specs=pl.BlockSpec((tm, tn), lambda m, n: (m, n)),
        out_shape=jax.ShapeDtypeStruct((M, width), BF16),
        compiler_params=_params("arbitrary", "arbitrary"),
        name="sconv",
    )(z, z, z, conv_w, conv_b.reshape(L, 1, width))


def _merge_kernel(a_ref, f_ref, c_ref, wa_ref, wf_ref, wc_ref, ga_ref, gf_ref, gc_ref, o_ref):
    a = _bdot(a_ref[...], wa_ref[...].astype(BF16))
    f = _bdot(f_ref[...], wf_ref[...].astype(BF16))
    c = _bdot(c_ref[...], wc_ref[...].astype(BF16))
    def sig(g_ref):
        return jax.nn.sigmoid(g_ref[...].astype(F32))

    o = sig(ga_ref) * a + sig(gf_ref) * f + sig(gc_ref) * c
    o_ref[...] = o.astype(o_ref.dtype)


def _merge(att, fr, cv, w_na_out, w_fnet_out, w_conv_out, z, l, tm=1024, tn=512):
    M = att.shape[0]
    D = w_na_out.shape[2]
    gb = D // tn

    def act(a):
        return _act_spec(tm, a.shape[1], True)

    def wgt(w):
        return pl.BlockSpec((None, w.shape[1], tn), lambda m, n: (l, 0, n))

    def gate(j):
        return pl.BlockSpec((tm, tn), lambda m, n: (m, j * gb + n))

    return pl.pallas_call(
        _merge_kernel,
        grid=(M // tm, D // tn),
        in_specs=[act(att), act(fr), act(cv), wgt(w_na_out), wgt(w_fnet_out), wgt(w_conv_out),
                  gate(0), gate(1), gate(2)],
        out_specs=pl.BlockSpec((tm, tn), lambda m, n: (m, n)),
        out_shape=jax.ShapeDtypeStruct((M, D), BF16),
        compiler_params=_params("arbitrary", "arbitrary"),
        name="merge",
    )(att, fr, cv, w_na_out, w_fnet_out, w_conv_out, z, z, z)


def _softmax_pv(s_list, v_list):
    m = functools.reduce(jnp.maximum, [jnp.max(s, axis=-1, keepdims=True) for s in s_list])
    e_list = [jnp.exp(s - m) for s in s_list]
    denom = functools.reduce(jnp.add, [jnp.sum(e, axis=-1, keepdims=True) for e in e_list])
    outs = [_bdot(e.astype(BF16), v) for e, v in zip(e_list, v_list)]
    return functools.reduce(jnp.add, outs) * (1.0 / denom)


def _qk(q, k):
    return lax.dot_general(q, k, (((1,), (1,)), ((), ())), preferred_element_type=F32)


def _ctx_attn_kernel(q_ref, k_ref, v_ref, o_ref, *, scale):
    for h in range(N_HEADS):
        sl = slice(h * HEAD_DIM, (h + 1) * HEAD_DIM)
        q = q_ref[:, sl].astype(BF16)
        k = k_ref[:, sl].astype(BF16)
        v = v_ref[:, sl].astype(BF16)
        s = _qk(q, k) * scale
        o_ref[:, sl] = _softmax_pv([s], [v]).astype(o_ref.dtype)


def _ctx_attention(q, new_k, new_v, l, out_rows):
    n_seq, _, seq, W = new_k.shape
    kern = functools.partial(_ctx_attn_kernel, scale=HEAD_DIM ** -0.5)
    kv_spec = pl.BlockSpec((None, None, seq, W), lambda b: (b, l, 0, 0))
    return pl.pallas_call(
        kern,
        grid=(n_seq,),
        in_specs=[pl.BlockSpec((seq, W), lambda b: (b, 0)), kv_spec, kv_spec],
        out_specs=pl.BlockSpec((seq, W), lambda b: (b, 0)),
        out_shape=jax.ShapeDtypeStruct((out_rows, W), BF16),
        compiler_params=_params("arbitrary"),
        name="ctx_attention",
    )(q, new_k, new_v)


def _window_starts(rows):
    kr = min(NA_ROWS, rows)
    return [min(max(r - kr // 2, 0), rows - kr) for r in range(rows)], kr


def _row_groups(rows):
    starts, kr = _window_starts(rows)
    groups = []
    for r0 in range(0, rows, NA_GROUP_ROWS):
        r1 = min(r0 + NA_GROUP_ROWS, rows)
        groups.append((r0, r1, min(starts[r0:r1]), max(starts[r0:r1]) + kr))
    return groups


def _na_attn_kernel(att_ref, q_ref, k_ref, v_ref, kc_ref, vc_ref, *rest, scale, rows):
    bias_refs, o_ref = rest[:-1], rest[-1]
    q = q_ref[...].astype(BF16)
    k = k_ref[...].astype(BF16)
    v = v_ref[...].astype(BF16)
    vc = vc_ref[...].astype(BF16)
    s_ctx = _qk(q, kc_ref[...].astype(BF16)) * scale
    for (r0, r1, k0, k1), bias_ref in zip(_row_groups(rows), bias_refs):
        qs = slice(r0 * GRID_W, r1 * GRID_W)
        ks = slice(k0 * GRID_W, k1 * GRID_W)
        s_lat = _qk(q[qs], k[ks]) * scale + bias_ref[...]
        o_ref[qs, :] = _softmax_pv([s_lat, s_ctx[qs]], [v[ks], vc]).astype(o_ref.dtype)


def _na_attention(att, q, kv, row0, n_seq, seq, cache_k, cache_v, biases, l):
    P = cache_k.shape[2]
    r0 = row0 // seq
    rows = seq // GRID_W
    kern = functools.partial(_na_attn_kernel, scale=HEAD_DIM ** -0.5, rows=rows)

    def kvspec(j):
        return pl.BlockSpec((seq, HEAD_DIM), lambda h, b: (b, j * N_HEADS + h))

    cspec = pl.BlockSpec((None, None, P, HEAD_DIM), lambda h, b: (b, l, 0, h))
    bspecs = [pl.BlockSpec((None, None) + bias.shape[2:], lambda h, b: (l, h, 0, 0)) for bias in biases]
    return pl.pallas_call(
        kern,
        grid=(N_HEADS, n_seq),
        in_specs=[pl.BlockSpec(memory_space=pl.ANY),
                  pl.BlockSpec((seq, HEAD_DIM), lambda h, b: (r0 + b, h)),
                  kvspec(0), kvspec(1), cspec, cspec] + bspecs,
        out_specs=pl.BlockSpec((seq, HEAD_DIM), lambda h, b: (r0 + b, h)),
        out_shape=jax.ShapeDtypeStruct(att.shape, att.dtype),
        input_output_aliases={0: 0},
        compiler_params=_params("arbitrary", "arbitrary"),
        name="na_attention",
    )(att, q, kv, kv, cache_k, cache_v, *biases)


def _na_bias(rpb, rows):
    starts, kr = _window_starts(rows)
    col = np.arange(GRID_W)
    cs = np.clip(col - NA_COLS // 2, 0, GRID_W - NA_COLS)
    col_ok = (col[None, :] >= cs[:, None]) & (col[None, :] < cs[:, None] + NA_COLS)
    col_idx = np.clip(col[None, :] - col[:, None] + NA_COLS - 1, 0, 2 * NA_COLS - 2)
    onehot = (col_idx[None] == np.arange(2 * NA_COLS - 1)[:, None, None]) & col_ok[None]
    e = jnp.einsum('lhrd,dqk->lhrqk', rpb.astype(F32), jnp.asarray(onehot, F32),
                   precision=lax.Precision.HIGHEST)
    e = jnp.where(jnp.asarray(col_ok), e, NEG_INF)
    biases = []
    for r0, r1, k0, k1 in _row_groups(rows):
        per_row = []
        for r in range(r0, r1):
            d0 = starts[r] - r + NA_ROWS - 1
            band = e[:, :, d0:d0 + kr].transpose(0, 1, 3, 2, 4)
            band = band.reshape(band.shape[:3] + (kr * GRID_W,))
            pad = ((starts[r] - k0) * GRID_W, (k1 - starts[r] - kr) * GRID_W)
            per_row.append(jnp.pad(band, ((0, 0), (0, 0), (0, 0), pad), constant_values=NEG_INF))
        biases.append(jnp.concatenate(per_row, axis=2))
    return biases


def _split_bf16(x):
    hi = x.astype(BF16)
    return hi, (x - hi.astype(F32)).astype(BF16)


def _dot_split(a, b):
    (ah, al), (bh, bl) = a, b
    return _bdot(ah, bh) + _bdot(ah, bl) + _bdot(al, bh)


def _fnet_kernel(x_ref, wch_ref, wcl_ref, wth_ref, wtl_ref, *rest):
    o_ref = rest[-1]
    t = _dot_split(_split_bf16(x_ref[...]), (wch_ref[...], wcl_ref[...]))
    g = t.shape[1] // 2
    u = jnp.concatenate([t[:, :g], t[:, g:]], axis=0)
    y = _dot_split((wth_ref[...], wtl_ref[...]), _split_bf16(u))
    o_ref[...] = y.astype(o_ref.dtype)


def _dft_mats(n):
    k = np.arange(n, dtype=np.int64)
    ang = 2.0 * np.pi * ((k[:, None] * k[None, :]) % n) / n
    s = 1.0 / np.sqrt(n)
    return np.cos(ang) * s, np.sin(ang) * s


def _split_const(w):
    w = jnp.asarray(w, F32)
    return _split_bf16(w)


def _fnet(z, col0, groups, gdim, row0, n_seq, seq, prev=None):
    ct, st = _dft_mats(seq)
    cc, sc = _dft_mats(gdim)
    wch, wcl = _split_const(np.concatenate([cc, sc], axis=1))
    wth, wtl = _split_const(np.concatenate([ct, -st], axis=1))
    r0 = row0 // seq
    c0 = col0 // gdim

    def const(w):
        return pl.BlockSpec(w.shape, lambda b, g: (0, 0))

    in_specs = [pl.BlockSpec((seq, gdim), lambda b, g: (r0 + b, c0 + g)),
                const(wch), const(wcl), const(wth), const(wtl)]
    args = [z, wch, wcl, wth, wtl]
    aliases = {}
    if prev is not None:
        in_specs.append(pl.BlockSpec(memory_space=pl.ANY))
        args.append(prev)
        aliases = {5: 0}
    return pl.pallas_call(
        _fnet_kernel,
        grid=(n_seq, groups),
        in_specs=in_specs,
        out_specs=pl.BlockSpec((seq, gdim), lambda b, g: (r0 + b, g)),
        out_shape=jax.ShapeDtypeStruct((z.shape[0], groups * gdim), BF16),
        input_output_aliases=aliases,
        compiler_params=_params("arbitrary", "arbitrary"),
        name="fnet",
    )(*args)


def kernel(x_prompt, x_sample, cache_k, cache_v, c, c_ctx, w_mod, b_mod, g_pre1, g_post1, g_pre2, g_post2,
           w_in, rpb, w_na_out, w_fnet_out, conv_w, conv_b, w_conv_out, w_o, w_up, ffn_conv_w, ffn_conv_b,
           w_down):
    B, S, D = x_prompt.shape
    Bd, T, _ = x_sample.shape
    L = w_mod.shape[0]
    P = cache_k.shape[2]
    W = N_HEADS * HEAD_DIM
    n_ctx = B * S
    n_lat = Bd * T
    fw = w_fnet_out.shape[1]
    cwid = w_conv_out.shape[1]
    fgroups = 4
    assert T == ROW_GROUP and n_ctx % ROW_GROUP == 0 and Bd + 1 <= 8

    x = jnp.concatenate([x_prompt.reshape(n_ctx, D), x_sample.reshape(n_lat, D)], axis=0)

    c_rows = jnp.zeros((8, D), F32).at[:Bd].set(c).at[Bd].set(c_ctx)
    mod = _modulation(c_rows, w_mod, b_mod)
    group_row = np.concatenate([np.full(n_ctx // ROW_GROUP, Bd), np.arange(Bd)])
    mods = mod.reshape(L, 8, 6, D)[:, group_row]
    mods = mods.transpose(0, 2, 1, 3)[:, :, :, None, :]

    gains = [g.reshape(L, 1, D) for g in (g_pre1, g_post1, g_pre2, g_post2)]
    ck = cache_k.reshape(Bd, L, P, W)
    cv_cache = cache_v.reshape(Bd, L, P, W)

    col_f = 3 * W
    col_gate = col_f + fw + 3 * cwid
    biases = _na_bias(rpb, T // GRID_W)

    new_k = new_v = None
    h = _prenorm(x, gains[0], mods, 0, 1, 0)
    for l in range(L):
        q = _matmul(h, w_in, l, BF16, 1024, 512, "proj_q", 0, W)
        new_k = _cache_proj(h, w_in, l, W, new_k, B, S, L, 1024, 512, "proj_k_ctx")
        new_v = _cache_proj(h, w_in, l, 2 * W, new_v, B, S, L, 1024, 512, "proj_v_ctx")
        kv_lat = _matmul(h, w_in, l, BF16, 1024, 512, "proj_kv_lat", W, 2 * W, row0=n_ctx)
        fxbc = _matmul(h, w_in, l, F32, 1024, 512, "proj_fxbc", col_f, col_gate - col_f)
        zg = _matmul(h, w_in, l, BF16, 1024, 512, "proj_gates", col_gate)
        att = _ctx_attention(q, new_k, new_v, l, n_ctx + n_lat)
        att = _na_attention(att, q, kv_lat, n_ctx, Bd, T, ck, cv_cache, biases, l)
        fr = _fnet(fxbc, 0, fgroups, fw // fgroups, 0, B, S)
        fr = _fnet(fxbc, 0, fgroups, fw // fgroups, n_ctx, Bd, T, prev=fr)
        cv = _sconv(fxbc, fw, cwid, conv_w, conv_b, l, n_ctx, S, T)
        merged = _merge(att, fr, cv, w_na_out, w_fnet_out, w_conv_out, zg, l)
        y = _matmul(merged, w_o, l, F32, 1024, 512, "proj_o")
        x, h = _resid(x, y, gains[1], mods, l, 2, (gains[2], l, 4, 3))
        act = _ffn_up(h, w_up, ffn_conv_w, ffn_conv_b, l, n_ctx, S, T)
        y = _matmul(act, w_down, l, F32, 1024, 256, "ffn_down", single_buffer=True)
        nxt = (gains[0], l + 1, 1, 0) if l + 1 < L else None
        x, h = _resid(x, y, gains[3], mods, l, 5, nxt)

    y_prompt = x[:n_ctx].reshape(B, S, D)
    y_sample = x[n_ctx:].reshape(Bd, T, D)
    kv_shape = (B, L, S, N_HEADS, HEAD_DIM)
    return y_prompt, y_sample, new_k.reshape(kv_shape), new_v.reshape(kv_shape)
```

```python
import functools

import numpy as np
import jax
import jax.numpy as jnp
from jax import lax
from jax.experimental import pallas as pl
from jax.experimental.pallas import tpu as pltpu

F32 = jnp.float32
BF16 = jnp.bfloat16

EPS = 1e-6
NEG_INF = -1e30
GRID_W = 64
NA_ROWS = 8
NA_COLS = 16
N_HEADS = 16
HEAD_DIM = 128
MERGE_CHUNK = 256
FFN_CHUNK = 512
NA_GROUP_ROWS = 4

VMEM_LIMIT_BYTES = 58 * 1024 * 1024
ROW_GROUP = 1024
PROJ_TM, PROJ_TN = 1024, 512


def _params(*sem):
    return pltpu.CompilerParams(dimension_semantics=sem, vmem_limit_bytes=VMEM_LIMIT_BYTES)


def _bdot(a, b):
    return jnp.dot(a, b, preferred_element_type=F32)


def _mod_kernel(c_ref, w_ref, b_ref, o_ref):
    c = c_ref[...]
    s = (c * jax.nn.sigmoid(c)).astype(BF16)
    o_ref[...] = _bdot(s, w_ref[...].astype(BF16)) + b_ref[...]


def _modulation(c_rows, w_mod, b_mod, tn=512):
    L, D, N = w_mod.shape
    return pl.pallas_call(
        _mod_kernel,
        grid=(L, N // tn),
        in_specs=[
            pl.BlockSpec((8, D), lambda l, n: (0, 0)),
            pl.BlockSpec((None, D, tn), lambda l, n: (l, 0, n)),
            pl.BlockSpec((None, 1, tn), lambda l, n: (l, 0, n)),
        ],
        out_specs=pl.BlockSpec((None, 8, tn), lambda l, n: (l, 0, n)),
        out_shape=jax.ShapeDtypeStruct((L, 8, N), F32),
        compiler_params=_params("arbitrary", "arbitrary"),
        name="modulation",
    )(c_rows, w_mod, b_mod.reshape(L, 1, N))


def _rms(x):
    return x * lax.rsqrt(jnp.mean(x * x, axis=-1, keepdims=True) + EPS)


def _prenorm_kernel(x_ref, g_ref, sc_ref, sh_ref, h_ref):
    h = _rms(x_ref[...]) * g_ref[...]
    h_ref[...] = (h * (1.0 + sc_ref[...]) + sh_ref[...]).astype(BF16)


def _resid_kernel(x_ref, y_ref, gpost_ref, gate_ref, gpre_ref, sc_ref, sh_ref, xo_ref, h_ref):
    x = x_ref[...] + gate_ref[...] * (_rms(y_ref[...]) * gpost_ref[...])
    xo_ref[...] = x
    h = _rms(x) * gpre_ref[...]
    h_ref[...] = (h * (1.0 + sc_ref[...]) + sh_ref[...]).astype(BF16)


def _resid_last_kernel(x_ref, y_ref, gpost_ref, gate_ref, xo_ref):
    xo_ref[...] = x_ref[...] + gate_ref[...] * (_rms(y_ref[...]) * gpost_ref[...])


def _row_spec(tr, D, i0=0):
    return pl.BlockSpec((tr, D), lambda i: (i0 + i, 0))


def _gain_spec(l, D):
    return pl.BlockSpec((None, 1, D), lambda i: (l, 0, 0))


def _mod_spec(l, j, tr, D, i0=0):
    return pl.BlockSpec((None, None, None, 1, D), lambda i: (l, j, ((i0 + i) * tr) // ROW_GROUP, 0, 0))


def _prenorm(x, gains, mods, l, j_sc, j_sh, tr=256):
    M, D = x.shape
    return pl.pallas_call(
        _prenorm_kernel,
        grid=(M // tr,),
        in_specs=[_row_spec(tr, D), _gain_spec(l, D), _mod_spec(l, j_sc, tr, D), _mod_spec(l, j_sh, tr, D)],
        out_specs=_row_spec(tr, D),
        out_shape=jax.ShapeDtypeStruct((M, D), BF16),
        compiler_params=_params("arbitrary"),
        name="prenorm",
    )(x, gains, mods, mods)


def _resid_last(x, y, gpost, mods, l, j_gate, row0, nrows, tr=256):
    D = x.shape[1]
    i0 = row0 // tr
    return pl.pallas_call(
        _resid_last_kernel,
        grid=(nrows // tr,),
        in_specs=[_row_spec(tr, D, i0), _row_spec(tr, D, i0), _gain_spec(l, D),
                  _mod_spec(l, j_gate, tr, D, i0)],
        out_specs=_row_spec(tr, D),
        out_shape=jax.ShapeDtypeStruct((nrows, D), F32),
        compiler_params=_params("arbitrary"),
        name="resid_last",
    )(x, y, gpost, mods)


def _resid(x, y, gpost, mods, l, j_gate, nxt, tr=256):
    M, D = x.shape
    ins = [_row_spec(tr, D), _row_spec(tr, D), _gain_spec(l, D), _mod_spec(l, j_gate, tr, D)]
    args = [x, y, gpost, mods]
    gpre, ln, j_sc, j_sh = nxt
    ins += [_gain_spec(ln, D), _mod_spec(ln, j_sc, tr, D), _mod_spec(ln, j_sh, tr, D)]
    args += [gpre, mods, mods]
    return pl.pallas_call(
        _resid_kernel,
        grid=(M // tr,),
        in_specs=ins,
        out_specs=[_row_spec(tr, D), _row_spec(tr, D)],
        out_shape=[jax.ShapeDtypeStruct((M, D), F32), jax.ShapeDtypeStruct((M, D), BF16)],
        compiler_params=_params("arbitrary"),
        name="resid",
    )(*args)


def _mm_kernel(x_ref, w_ref, o_ref):
    o_ref[...] = _bdot(x_ref[...], w_ref[...].astype(BF16)).astype(o_ref.dtype)


def _act_spec(tm, K, single_buffer, m0=0):
    if single_buffer:
        return pl.BlockSpec((tm, K), lambda m, n: (m0 + m, 0), pipeline_mode=pl.Buffered(1))
    return pl.BlockSpec((tm, K), lambda m, n: (m0 + m, 0))


def _matmul(x, w, l, out_dtype, tm, tn, name, col0=0, ncols=None, row0=0, nrows=None, single_buffer=False):
    K = x.shape[1]
    M = x.shape[0] - row0 if nrows is None else nrows
    N = w.shape[2] - col0 if ncols is None else ncols
    c0 = col0 // tn
    return pl.pallas_call(
        _mm_kernel,
        grid=(M // tm, N // tn),
        in_specs=[
            _act_spec(tm, K, single_buffer, row0 // tm),
            pl.BlockSpec((None, K, tn), lambda m, n: (l, 0, c0 + n)),
        ],
        out_specs=pl.BlockSpec((tm, tn), lambda m, n: (m, n)),
        out_shape=jax.ShapeDtypeStruct((M, N), out_dtype),
        compiler_params=_params("arbitrary", "arbitrary"),
        name=name,
    )(x, w)


def _cache_proj_kernel(x_ref, wk_ref, wv_ref, *rest):
    ko_ref, vo_ref = rest[-2:]
    x = x_ref[...]
    ko_ref[...] = _bdot(x, wk_ref[...].astype(BF16)).reshape(ko_ref.shape)
    vo_ref[...] = _bdot(x, wv_ref[...].astype(BF16)).reshape(vo_ref.shape)


def _cache_proj(h, w, l, col_k, col_v, caches, n_seq, seq, depth, tm, tn):
    K = h.shape[1]
    W = N_HEADS * HEAD_DIM
    ck0, cv0 = col_k // tn, col_v // tn
    spt = tm // seq
    in_specs = [
        pl.BlockSpec((tm, K), lambda m, n: (m, 0)),
        pl.BlockSpec((None, K, tn), lambda m, n: (l, 0, ck0 + n)),
        pl.BlockSpec((None, K, tn), lambda m, n: (l, 0, cv0 + n)),
    ]
    args = [h, w, w]
    aliases = {}
    if caches is not None:
        in_specs += [pl.BlockSpec(memory_space=pl.ANY)] * 2
        args += list(caches)
        aliases = {3: 0, 4: 1}
    out_spec = pl.BlockSpec((spt, None, seq, tn), lambda m, n: (m, l, 0, n))
    out_shape = jax.ShapeDtypeStruct((n_seq, depth, seq, W), F32)
    return pl.pallas_call(
        _cache_proj_kernel,
        grid=(n_seq * seq // tm, W // tn),
        in_specs=in_specs,
        out_specs=[out_spec, out_spec],
        out_shape=[out_shape, out_shape],
        input_output_aliases=aliases,
        compiler_params=_params("arbitrary", "arbitrary"),
        name="proj_kv_ctx",
    )(*args)


def _seq_pos(tm, n_ctx_tiles, seq_ctx, seq_lat):
    seq = jnp.where(pl.program_id(0) < n_ctx_tiles, seq_ctx, seq_lat)
    row = lax.broadcasted_iota(jnp.int32, (tm, 1), 0)
    return row & (seq - 1), seq


def _dwconv3(u, pos, seq, w_ref, b_ref):
    tm = u.shape[0]
    prev = jnp.where(pos == 0, 0.0, pltpu.roll(u, 1, 0))
    nxt = jnp.where(pos == seq - 1, 0.0, pltpu.roll(u, tm - 1, 0))
    return prev * w_ref[0:1, :] + u * w_ref[1:2, :] + nxt * w_ref[2:3, :] + b_ref[...]


def _ffn_up_kernel(h_ref, wg_ref, wv_ref, cw_ref, cb_ref, o_ref, *, n_ctx_tiles, seq_ctx, seq_lat):
    tm = h_ref.shape[0]
    wg = wg_ref[...].astype(BF16)
    wv = wv_ref[...].astype(BF16)
    seq = jnp.where(pl.program_id(0) < n_ctx_tiles, seq_ctx, seq_lat)
    gate = jnp.concatenate([_bdot(h_ref[r0:r0 + FFN_CHUNK, :], wg) for r0 in range(0, tm, FFN_CHUNK)], axis=0)
    for r0 in range(0, tm, FFN_CHUNK):
        val = _bdot(h_ref[r0:r0 + FFN_CHUNK, :], wv)
        lo, hi = max(r0 - 8, 0), min(r0 + FFN_CHUNK + 8, tm)
        pos = (lo + lax.broadcasted_iota(jnp.int32, (hi - lo, 1), 0)) & (seq - 1)
        g = _dwconv3(gate[lo:hi], pos, seq, cw_ref, cb_ref)[r0 - lo:r0 - lo + FFN_CHUNK]
        o_ref[r0:r0 + FFN_CHUNK, :] = (g * jax.nn.sigmoid(g) * val).astype(o_ref.dtype)


def _ffn_up(h, w_up, conv_w, conv_b, l, n_ctx_rows, seq_ctx, seq_lat, tm=1024, tn=256):
    M, K = h.shape
    L, _, N2 = w_up.shape
    F = N2 // 2
    nb = F // tn
    kern = functools.partial(_ffn_up_kernel, n_ctx_tiles=n_ctx_rows // tm, seq_ctx=seq_ctx, seq_lat=seq_lat)
    return pl.pallas_call(
        kern,
        grid=(M // tm, nb),
        in_specs=[
            pl.BlockSpec((tm, K), lambda m, n: (m, 0)),
            pl.BlockSpec((None, K, tn), lambda m, n: (l, 0, n)),
            pl.BlockSpec((None, K, tn), lambda m, n: (l, 0, nb + n)),
            pl.BlockSpec((None, 3, tn), lambda m, n: (l, 0, n)),
            pl.BlockSpec((None, 1, tn), lambda m, n: (l, 0, n)),
        ],
        out_specs=pl.BlockSpec((tm, tn), lambda m, n: (m, n)),
        out_shape=jax.ShapeDtypeStruct((M, F), BF16),
        compiler_params=_params("arbitrary", "arbitrary"),
        name="ffn_up",
    )(h, w_up, w_up, conv_w, conv_b.reshape(L, 1, F))


def _sconv_kernel(zx_ref, zb_ref, zc_ref, cw_ref, cb_ref, o_ref, *, n_ctx_tiles, seq_ctx, seq_lat):
    u = zc_ref[...] * zx_ref[...]
    pos, seq = _seq_pos(u.shape[0], n_ctx_tiles, seq_ctx, seq_lat)
    o_ref[...] = (zb_ref[...] * _dwconv3(u, pos, seq, cw_ref, cb_ref)).astype(o_ref.dtype)


def _sconv(z, col0, width, conv_w, conv_b, l, n_ctx_rows, seq_ctx, seq_lat, tm=1024, tn=512):
    M = z.shape[0]
    L = conv_w.shape[0]
    kern = functools.partial(_sconv_kernel, n_ctx_tiles=n_ctx_rows // tm, seq_ctx=seq_ctx, seq_lat=seq_lat)
    cb0 = col0 // tn
    wb = width // tn
    return pl.pallas_call(
        kern,
        grid=(M // tm, wb),
        in_specs=[
            pl.BlockSpec((tm, tn), lambda m, n: (m, cb0 + n)),
            pl.BlockSpec((tm, tn), lambda m, n: (m, cb0 + wb + n)),
            pl.BlockSpec((tm, tn), lambda m, n: (m, cb0 + 2 * wb + n)),
            pl.BlockSpec((None, 3, tn), lambda m, n: (l, 0, n)),
            pl.BlockSpec((None, 1, tn), lambda m, n: (l, 0, n)),
        ],
        out_specs=pl.BlockSpec((tm, tn), lambda m, n: (m, n)),
        out_shape=jax.ShapeDtypeStruct((M, width), BF16),
        compiler_params=_params("arbitrary", "arbitrary"),
        name="sconv",
    )(z, z, z, conv_w, conv_b.reshape(L, 1, width))


def _merge_kernel(a_ref, f_ref, c_ref, wa_ref, wf_ref, wc_ref, ga_ref, gf_ref, gc_ref, o_ref):
    wa = wa_ref[...].astype(BF16)
    wf = wf_ref[...].astype(BF16)
    wc = wc_ref[...].astype(BF16)
    for r0 in range(0, o_ref.shape[0], MERGE_CHUNK):
        rows = slice(r0, r0 + MERGE_CHUNK)

        def gated(x_ref, w, g_ref):
            return jax.nn.sigmoid(g_ref[rows, :].astype(F32)) * _bdot(x_ref[rows, :], w)

        o = gated(a_ref, wa, ga_ref) + gated(f_ref, wf, gf_ref) + gated(c_ref, wc, gc_ref)
        o_ref[rows, :] = o.astype(o_ref.dtype)


def _merge(att, fr, cv, w_na_out, w_fnet_out, w_conv_out, z, l, tm=1024, tn=512):
    M = att.shape[0]
    D = w_na_out.shape[2]
    gb = D // tn

    def act(a):
        return _act_spec(tm, a.shape[1], True)

    def wgt(w):
        return pl.BlockSpec((None, w.shape[1], tn), lambda m, n: (l, 0, n))

    def gate(j):
        return pl.BlockSpec((tm, tn), lambda m, n: (m, j * gb + n))

    return pl.pallas_call(
        _merge_kernel,
        grid=(M // tm, D // tn),
        in_specs=[act(att), act(fr), act(cv), wgt(w_na_out), wgt(w_fnet_out), wgt(w_conv_out),
                  gate(0), gate(1), gate(2)],
        out_specs=pl.BlockSpec((tm, tn), lambda m, n: (m, n)),
        out_shape=jax.ShapeDtypeStruct((M, D), BF16),
        compiler_params=_params("arbitrary", "arbitrary"),
        name="merge",
    )(att, fr, cv, w_na_out, w_fnet_out, w_conv_out, z, z, z)


def _softmax_pv(s_list, v_list):
    m = functools.reduce(jnp.maximum, [jnp.max(s, axis=-1, keepdims=True) for s in s_list])
    e_list = [jnp.exp(s - m) for s in s_list]
    denom = functools.reduce(jnp.add, [jnp.sum(e, axis=-1, keepdims=True) for e in e_list])
    outs = [_bdot(e.astype(BF16), v) for e, v in zip(e_list, v_list)]
    return functools.reduce(jnp.add, outs) * (1.0 / denom)


def _qk(q, k):
    return lax.dot_general(q, k, (((1,), (1,)), ((), ())), preferred_element_type=F32)


def _ctx_attn_kernel(q_ref, k_ref, v_ref, o_ref, *, scale):
    for h in range(N_HEADS):
        sl = slice(h * HEAD_DIM, (h + 1) * HEAD_DIM)
        q = q_ref[:, sl].astype(BF16)
        k = k_ref[:, sl].astype(BF16)
        v = v_ref[:, sl].astype(BF16)
        s = _qk(q, k) * scale
        o_ref[:, sl] = _softmax_pv([s], [v]).astype(o_ref.dtype)


def _ctx_attention(q, new_k, new_v, l, out_rows):
    n_seq, _, seq, W = new_k.shape
    kern = functools.partial(_ctx_attn_kernel, scale=HEAD_DIM ** -0.5)
    kv_spec = pl.BlockSpec((None, None, seq, W), lambda b: (b, l, 0, 0))
    return pl.pallas_call(
        kern,
        grid=(n_seq,),
        in_specs=[pl.BlockSpec((seq, W), lambda b: (b, 0)), kv_spec, kv_spec],
        out_specs=pl.BlockSpec((seq, W), lambda b: (b, 0)),
        out_shape=jax.ShapeDtypeStruct((out_rows, W), BF16),
        compiler_params=_params("arbitrary"),
        name="ctx_attention",
    )(q, new_k, new_v)


def _window_starts(rows):
    kr = min(NA_ROWS, rows)
    return [min(max(r - kr // 2, 0), rows - kr) for r in range(rows)], kr


def _row_groups(rows):
    starts, kr = _window_starts(rows)
    groups = []
    for r0 in range(0, rows, NA_GROUP_ROWS):
        r1 = min(r0 + NA_GROUP_ROWS, rows)
        groups.append((r0, r1, min(starts[r0:r1]), max(starts[r0:r1]) + kr))
    return groups


def _na_attn_kernel(att_ref, q_ref, k_ref, v_ref, kc_ref, vc_ref, *rest, scale, rows):
    bias_refs, o_ref = rest[:-1], rest[-1]
    q = q_ref[...].astype(BF16)
    k = k_ref[...].astype(BF16)
    v = v_ref[...].astype(BF16)
    vc = vc_ref[...].astype(BF16)
    s_ctx = _qk(q, kc_ref[...].astype(BF16)) * scale
    for (r0, r1, k0, k1), bias_ref in zip(_row_groups(rows), bias_refs):
        qs = slice(r0 * GRID_W, r1 * GRID_W)
        ks = slice(k0 * GRID_W, k1 * GRID_W)
        s_lat = _qk(q[qs], k[ks]) * scale + bias_ref[...]
        o_ref[qs, :] = _softmax_pv([s_lat, s_ctx[qs]], [v[ks], vc]).astype(o_ref.dtype)


def _na_attention(att, q, kv, row0, n_seq, seq, cache_k, cache_v, biases, l):
    P = cache_k.shape[2]
    r0 = row0 // seq
    rows = seq // GRID_W
    kern = functools.partial(_na_attn_kernel, scale=HEAD_DIM ** -0.5, rows=rows)

    def kvspec(j):
        return pl.BlockSpec((seq, HEAD_DIM), lambda h, b: (b, j * N_HEADS + h))

    cspec = pl.BlockSpec((None, None, P, HEAD_DIM), lambda h, b: (b, l, 0, h))
    bspecs = [pl.BlockSpec((None, None) + bias.shape[2:], lambda h, b: (l, h, 0, 0)) for bias in biases]
    return pl.pallas_call(
        kern,
        grid=(N_HEADS, n_seq),
        in_specs=[pl.BlockSpec(memory_space=pl.ANY),
                  pl.BlockSpec((seq, HEAD_DIM), lambda h, b: (r0 + b, h)),
                  kvspec(0), kvspec(1), cspec, cspec] + bspecs,
        out_specs=pl.BlockSpec((seq, HEAD_DIM), lambda h, b: (r0 + b, h)),
        out_shape=jax.ShapeDtypeStruct(att.shape, att.dtype),
        input_output_aliases={0: 0},
        compiler_params=_params("arbitrary", "arbitrary"),
        name="na_attention",
    )(att, q, kv, kv, cache_k, cache_v, *biases)


def _na_bias(rpb, rows):
    starts, kr = _window_starts(rows)
    col = np.arange(GRID_W)
    cs = np.clip(col - NA_COLS // 2, 0, GRID_W - NA_COLS)
    col_ok = (col[None, :] >= cs[:, None]) & (col[None, :] < cs[:, None] + NA_COLS)
    col_idx = np.clip(col[None, :] - col[:, None] + NA_COLS - 1, 0, 2 * NA_COLS - 2)
    onehot = (col_idx[None] == np.arange(2 * NA_COLS - 1)[:, None, None]) & col_ok[None]
    e = jnp.einsum('lhrd,dqk->lhrqk', rpb.astype(F32), jnp.asarray(onehot, F32),
                   precision=lax.Precision.HIGHEST)
    e = jnp.where(jnp.asarray(col_ok), e, NEG_INF)
    biases = []
    for r0, r1, k0, k1 in _row_groups(rows):
        per_row = []
        for r in range(r0, r1):
            d0 = starts[r] - r + NA_ROWS - 1
            band = e[:, :, d0:d0 + kr].transpose(0, 1, 3, 2, 4)
            band = band.reshape(band.shape[:3] + (kr * GRID_W,))
            pad = ((starts[r] - k0) * GRID_W, (k1 - starts[r] - kr) * GRID_W)
            per_row.append(jnp.pad(band, ((0, 0), (0, 0), (0, 0), pad), constant_values=NEG_INF))
        biases.append(jnp.concatenate(per_row, axis=2))
    return biases


def _split_bf16(x):
    hi = x.astype(BF16)
    return hi, (x - hi.astype(F32)).astype(BF16)


def _dot_split(a, b):
    (ah, al), (bh, bl) = a, b
    return _bdot(ah, bh) + _bdot(ah, bl) + _bdot(al, bh)


def _fnet_kernel(x_ref, wch_ref, wcl_ref, wth_ref, wtl_ref, *rest):
    o_ref = rest[-1]
    t = _dot_split(_split_bf16(x_ref[...]), (wch_ref[...], wcl_ref[...]))
    g = t.shape[1] // 2
    u = jnp.concatenate([t[:, :g], t[:, g:]], axis=0)
    y = _dot_split((wth_ref[...], wtl_ref[...]), _split_bf16(u))
    o_ref[...] = y.astype(o_ref.dtype)


def _dft_mats(n):
    k = np.arange(n, dtype=np.int64)
    ang = 2.0 * np.pi * ((k[:, None] * k[None, :]) % n) / n
    s = 1.0 / np.sqrt(n)
    return np.cos(ang) * s, np.sin(ang) * s


def _split_const(w):
    w = jnp.asarray(w, F32)
    return _split_bf16(w)


def _fnet(z, col0, groups, gdim, row0, n_seq, seq, prev=None):
    ct, st = _dft_mats(seq)
    cc, sc = _dft_mats(gdim)
    wch, wcl = _split_const(np.concatenate([cc, sc], axis=1))
    wth, wtl = _split_const(np.concatenate([ct, -st], axis=1))
    r0 = row0 // seq
    c0 = col0 // gdim

    def const(w):
        return pl.BlockSpec(w.shape, lambda b, g: (0, 0))

    in_specs = [pl.BlockSpec((seq, gdim), lambda b, g: (r0 + b, c0 + g)),
                const(wch), const(wcl), const(wth), const(wtl)]
    args = [z, wch, wcl, wth, wtl]
    aliases = {}
    if prev is not None:
        in_specs.append(pl.BlockSpec(memory_space=pl.ANY))
        args.append(prev)
        aliases = {5: 0}
    return pl.pallas_call(
        _fnet_kernel,
        grid=(n_seq, groups),
        in_specs=in_specs,
        out_specs=pl.BlockSpec((seq, gdim), lambda b, g: (r0 + b, g)),
        out_shape=jax.ShapeDtypeStruct((z.shape[0], groups * gdim), BF16),
        input_output_aliases=aliases,
        compiler_params=_params("arbitrary", "arbitrary"),
        name="fnet",
    )(*args)


def kernel(x_prompt, x_sample, cache_k, cache_v, c, c_ctx, w_mod, b_mod, g_pre1, g_post1, g_pre2, g_post2,
           w_in, rpb, w_na_out, w_fnet_out, conv_w, conv_b, w_conv_out, w_o, w_up, ffn_conv_w, ffn_conv_b,
           w_down):
    B, S, D = x_prompt.shape
    Bd, T, _ = x_sample.shape
    L = w_mod.shape[0]
    P = cache_k.shape[2]
    W = N_HEADS * HEAD_DIM
    n_ctx = B * S
    n_lat = Bd * T
    fw = w_fnet_out.shape[1]
    cwid = w_conv_out.shape[1]
    fgroups = 4
    assert T == ROW_GROUP and n_ctx % ROW_GROUP == 0 and Bd + 1 <= 8

    x = jnp.concatenate([x_prompt.reshape(n_ctx, D), x_sample.reshape(n_lat, D)], axis=0)

    c_rows = jnp.zeros((8, D), F32).at[:Bd].set(c).at[Bd].set(c_ctx)
    mod = _modulation(c_rows, w_mod, b_mod)
    group_row = np.concatenate([np.full(n_ctx // ROW_GROUP, Bd), np.arange(Bd)])
    mods = mod.reshape(L, 8, 6, D)[:, group_row]
    mods = mods.transpose(0, 2, 1, 3)[:, :, :, None, :]

    gains = [g.reshape(L, 1, D) for g in (g_pre1, g_post1, g_pre2, g_post2)]
    ck = cache_k.reshape(Bd, L, P, W)
    cv_cache = cache_v.reshape(Bd, L, P, W)

    col_f = 3 * W
    col_gate = col_f + fw + 3 * cwid
    biases = _na_bias(rpb, T // GRID_W)

    new_kv = None
    h = _prenorm(x, gains[0], mods, 0, 1, 0)
    for l in range(L):
        tm, tn = PROJ_TM, PROJ_TN
        q = _matmul(h, w_in, l, BF16, tm, tn, "proj_q", 0, W)
        new_kv = _cache_proj(h, w_in, l, W, 2 * W, new_kv, B, S, L, tm, tn // 2)
        new_k, new_v = new_kv
        kv_lat = _matmul(h, w_in, l, BF16, tm, tn, "proj_kv_lat", W, 2 * W, row0=n_ctx)
        fxbc = _matmul(h, w_in, l, F32, tm, tn, "proj_fxbc", col_f, col_gate - col_f)
        zg = _matmul(h, w_in, l, BF16, tm, tn, "proj_gates", col_gate)
        att = _ctx_attention(q, new_k, new_v, l, n_ctx + n_lat)
        att = _na_attention(att, q, kv_lat, n_ctx, Bd, T, ck, cv_cache, biases, l)
        fr = _fnet(fxbc, 0, fgroups, fw // fgroups, 0, B, S)
        fr = _fnet(fxbc, 0, fgroups, fw // fgroups, n_ctx, Bd, T, prev=fr)
        cv = _sconv(fxbc, fw, cwid, conv_w, conv_b, l, n_ctx, S, T)
        merged = _merge(att, fr, cv, w_na_out, w_fnet_out, w_conv_out, zg, l)
        y = _matmul(merged, w_o, l, F32, tm, tn, "proj_o")
        x, h = _resid(x, y, gains[1], mods, l, 2, (gains[2], l, 4, 3))
        act = _ffn_up(h, w_up, ffn_conv_w, ffn_conv_b, l, n_ctx, S, T)
        y = _matmul(act, w_down, l, F32, 1024, 256, "ffn_down", single_buffer=True)
        if l + 1 < L:
            x, h = _resid(x, y, gains[3], mods, l, 5, (gains[0], l + 1, 1, 0))

    y_prompt = _resid_last(x, y, gains[3], mods, L - 1, 5, 0, n_ctx).reshape(B, S, D)
    y_sample = _resid_last(x, y, gains[3], mods, L - 1, 5, n_ctx, n_lat).reshape(Bd, T, D)
    kv_shape = (B, L, S, N_HEADS, HEAD_DIM)
    return y_prompt, y_sample, new_k.reshape(kv_shape), new_v.reshape(kv_shape)
```

```python
import functools

import numpy as np
import jax
import jax.numpy as jnp
from jax import lax
from jax.experimental import pallas as pl
from jax.experimental.pallas import tpu as pltpu

F32 = jnp.float32
BF16 = jnp.bfloat16

EPS = 1e-6
NEG_INF = -1e30
GRID_W = 64
NA_ROWS = 8
NA_COLS = 16
N_HEADS = 16
HEAD_DIM = 128
MERGE_CHUNK = 256
FFN_SPLITS = (0.75,)
NA_GROUP_ROWS = 4

VMEM_LIMIT_BYTES = 58 * 1024 * 1024
ROW_GROUP = 1024
PROJ_TM, PROJ_TN = 1024, 512


def _params(*sem):
    return pltpu.CompilerParams(dimension_semantics=sem, vmem_limit_bytes=VMEM_LIMIT_BYTES)


def _bdot(a, b):
    return jnp.dot(a, b, preferred_element_type=F32)


def _mod_kernel(c_ref, w_ref, b_ref, o_ref):
    c = c_ref[...]
    s = (c * jax.nn.sigmoid(c)).astype(BF16)
    o_ref[...] = _bdot(s, w_ref[...].astype(BF16)) + b_ref[...]


def _modulation(c_rows, w_mod, b_mod, tn=512):
    L, D, N = w_mod.shape
    return pl.pallas_call(
        _mod_kernel,
        grid=(L, N // tn),
        in_specs=[
            pl.BlockSpec((8, D), lambda l, n: (0, 0)),
            pl.BlockSpec((None, D, tn), lambda l, n: (l, 0, n)),
            pl.BlockSpec((None, 1, tn), lambda l, n: (l, 0, n)),
        ],
        out_specs=pl.BlockSpec((None, 8, tn), lambda l, n: (l, 0, n)),
        out_shape=jax.ShapeDtypeStruct((L, 8, N), F32),
        compiler_params=_params("arbitrary", "arbitrary"),
        name="modulation",
    )(c_rows, w_mod, b_mod.reshape(L, 1, N))


def _rms(x):
    return x * lax.rsqrt(jnp.mean(x * x, axis=-1, keepdims=True) + EPS)


def _prenorm_kernel(x_ref, g_ref, sc_ref, sh_ref, h_ref):
    h = _rms(x_ref[...]) * g_ref[...]
    h_ref[...] = (h * (1.0 + sc_ref[...]) + sh_ref[...]).astype(BF16)


def _resid_kernel(x_ref, y_ref, gpost_ref, gate_ref, gpre_ref, sc_ref, sh_ref, xo_ref, h_ref):
    x = x_ref[...] + gate_ref[...] * (_rms(y_ref[...]) * gpost_ref[...])
    xo_ref[...] = x
    h = _rms(x) * gpre_ref[...]
    h_ref[...] = (h * (1.0 + sc_ref[...]) + sh_ref[...]).astype(BF16)


def _resid_last_kernel(x_ref, y_ref, gpost_ref, gate_ref, xo_ref):
    xo_ref[...] = x_ref[...] + gate_ref[...] * (_rms(y_ref[...]) * gpost_ref[...])


def _row_spec(tr, D, i0=0):
    return pl.BlockSpec((tr, D), lambda i: (i0 + i, 0))


def _gain_spec(l, D):
    return pl.BlockSpec((None, 1, D), lambda i: (l, 0, 0))


def _mod_spec(l, j, tr, D, i0=0):
    return pl.BlockSpec((None, None, None, 1, D), lambda i: (l, j, ((i0 + i) * tr) // ROW_GROUP, 0, 0))


def _prenorm(x, gains, mods, l, j_sc, j_sh, tr=256):
    M, D = x.shape
    return pl.pallas_call(
        _prenorm_kernel,
        grid=(M // tr,),
        in_specs=[_row_spec(tr, D), _gain_spec(l, D), _mod_spec(l, j_sc, tr, D), _mod_spec(l, j_sh, tr, D)],
        out_specs=_row_spec(tr, D),
        out_shape=jax.ShapeDtypeStruct((M, D), BF16),
        compiler_params=_params("arbitrary"),
        name="prenorm",
    )(x, gains, mods, mods)


def _resid_last(x, y, gpost, mods, l, j_gate, row0, nrows, tr=256):
    D = x.shape[1]
    i0 = row0 // tr
    return pl.pallas_call(
        _resid_last_kernel,
        grid=(nrows // tr,),
        in_specs=[_row_spec(tr, D, i0), _row_spec(tr, D, i0), _gain_spec(l, D),
                  _mod_spec(l, j_gate, tr, D, i0)],
        out_specs=_row_spec(tr, D),
        out_shape=jax.ShapeDtypeStruct((nrows, D), F32),
        compiler_params=_params("arbitrary"),
        name="resid_last",
    )(x, y, gpost, mods)


def _resid(x, y, gpost, mods, l, j_gate, nxt, tr=256):
    M, D = x.shape
    ins = [_row_spec(tr, D), _row_spec(tr, D), _gain_spec(l, D), _mod_spec(l, j_gate, tr, D)]
    args = [x, y, gpost, mods]
    gpre, ln, j_sc, j_sh = nxt
    ins += [_gain_spec(ln, D), _mod_spec(ln, j_sc, tr, D), _mod_spec(ln, j_sh, tr, D)]
    args += [gpre, mods, mods]
    return pl.pallas_call(
        _resid_kernel,
        grid=(M // tr,),
        in_specs=ins,
        out_specs=[_row_spec(tr, D), _row_spec(tr, D)],
        out_shape=[jax.ShapeDtypeStruct((M, D), F32), jax.ShapeDtypeStruct((M, D), BF16)],
        compiler_params=_params("arbitrary"),
        name="resid",
    )(*args)


def _mm_kernel(x_ref, w_ref, o_ref):
    o_ref[...] = _bdot(x_ref[...], w_ref[...].astype(BF16)).astype(o_ref.dtype)


def _act_spec(tm, K, single_buffer, m0=0):
    if single_buffer:
        return pl.BlockSpec((tm, K), lambda m, n: (m0 + m, 0), pipeline_mode=pl.Buffered(1))
    return pl.BlockSpec((tm, K), lambda m, n: (m0 + m, 0))


def _matmul(x, w, l, out_dtype, tm, tn, name, col0=0, ncols=None, row0=0, nrows=None, single_buffer=False):
    K = x.shape[1]
    M = x.shape[0] - row0 if nrows is None else nrows
    N = w.shape[2] - col0 if ncols is None else ncols
    c0 = col0 // tn
    return pl.pallas_call(
        _mm_kernel,
        grid=(M // tm, N // tn),
        in_specs=[
            _act_spec(tm, K, single_buffer, row0 // tm),
            pl.BlockSpec((None, K, tn), lambda m, n: (l, 0, c0 + n)),
        ],
        out_specs=pl.BlockSpec((tm, tn), lambda m, n: (m, n)),
        out_shape=jax.ShapeDtypeStruct((M, N), out_dtype),
        compiler_params=_params("arbitrary", "arbitrary"),
        name=name,
    )(x, w)


def _cache_proj_kernel(x_ref, wk_ref, wv_ref, *rest):
    ko_ref, vo_ref = rest[-2:]
    x = x_ref[...]
    ko_ref[...] = _bdot(x, wk_ref[...].astype(BF16)).reshape(ko_ref.shape)
    vo_ref[...] = _bdot(x, wv_ref[...].astype(BF16)).reshape(vo_ref.shape)


def _cache_proj(h, w, l, col_k, col_v, caches, n_seq, seq, depth, tm, tn):
    K = h.shape[1]
    W = N_HEADS * HEAD_DIM
    ck0, cv0 = col_k // tn, col_v // tn
    spt = tm // seq
    in_specs = [
        pl.BlockSpec((tm, K), lambda m, n: (m, 0)),
        pl.BlockSpec((None, K, tn), lambda m, n: (l, 0, ck0 + n)),
        pl.BlockSpec((None, K, tn), lambda m, n: (l, 0, cv0 + n)),
    ]
    args = [h, w, w]
    aliases = {}
    if caches is not None:
        in_specs += [pl.BlockSpec(memory_space=pl.ANY)] * 2
        args += list(caches)
        aliases = {3: 0, 4: 1}
    out_spec = pl.BlockSpec((spt, None, seq, tn), lambda m, n: (m, l, 0, n))
    out_shape = jax.ShapeDtypeStruct((n_seq, depth, seq, W), F32)
    return pl.pallas_call(
        _cache_proj_kernel,
        grid=(n_seq * seq // tm, W // tn),
        in_specs=in_specs,
        out_specs=[out_spec, out_spec],
        out_shape=[out_shape, out_shape],
        input_output_aliases=aliases,
        compiler_params=_params("arbitrary", "arbitrary"),
        name="proj_kv_ctx",
    )(*args)


def _seq_pos(tm, n_ctx_tiles, seq_ctx, seq_lat):
    seq = jnp.where(pl.program_id(0) < n_ctx_tiles, seq_ctx, seq_lat)
    row = lax.broadcasted_iota(jnp.int32, (tm, 1), 0)
    return row & (seq - 1), seq


def _dwconv3(u, pos, seq, w_ref, b_ref):
    tm = u.shape[0]
    prev = jnp.where(pos == 0, 0.0, pltpu.roll(u, 1, 0))
    nxt = jnp.where(pos == seq - 1, 0.0, pltpu.roll(u, tm - 1, 0))
    return prev * w_ref[0:1, :] + u * w_ref[1:2, :] + nxt * w_ref[2:3, :] + b_ref[...]


def _ffn_up_kernel(h_ref, wg_ref, wv_ref, cw_ref, cb_ref, o_ref, *, n_ctx_tiles, seq_ctx, seq_lat):
    tm = h_ref.shape[0]
    wg = wg_ref[...].astype(BF16)
    wv = wv_ref[...].astype(BF16)
    seq = jnp.where(pl.program_id(0) < n_ctx_tiles, seq_ctx, seq_lat)
    bounds = [0] + [int(tm * f) for f in FFN_SPLITS] + [tm]
    gate = jnp.concatenate([_bdot(h_ref[r0:r1, :], wg) for r0, r1 in zip(bounds[:-1], bounds[1:])], axis=0)
    for r0, r1 in zip(bounds[:-1], bounds[1:]):
        val = _bdot(h_ref[r0:r1, :], wv)
        lo, hi = max(r0 - 8, 0), min(r1 + 8, tm)
        pos = (lo + lax.broadcasted_iota(jnp.int32, (hi - lo, 1), 0)) & (seq - 1)
        g = _dwconv3(gate[lo:hi], pos, seq, cw_ref, cb_ref)[r0 - lo:r1 - lo]
        o_ref[r0:r1, :] = (g * jax.nn.sigmoid(g) * val).astype(o_ref.dtype)


def _ffn_up(h, w_up, conv_w, conv_b, l, n_ctx_rows, seq_ctx, seq_lat, tm=1024, tn=256):
    M, K = h.shape
    L, _, N2 = w_up.shape
    F = N2 // 2
    nb = F // tn
    kern = functools.partial(_ffn_up_kernel, n_ctx_tiles=n_ctx_rows // tm, seq_ctx=seq_ctx, seq_lat=seq_lat)
    return pl.pallas_call(
        kern,
        grid=(M // tm, nb),
        in_specs=[
            pl.BlockSpec((tm, K), lambda m, n: (m, 0)),
            pl.BlockSpec((None, K, tn), lambda m, n: (l, 0, n)),
            pl.BlockSpec((None, K, tn), lambda m, n: (l, 0, nb + n)),
            pl.BlockSpec((None, 3, tn), lambda m, n: (l, 0, n)),
            pl.BlockSpec((None, 1, tn), lambda m, n: (l, 0, n)),
        ],
        out_specs=pl.BlockSpec((tm, tn), lambda m, n: (m, n)),
        out_shape=jax.ShapeDtypeStruct((M, F), BF16),
        compiler_params=_params("arbitrary", "arbitrary"),
        name="ffn_up",
    )(h, w_up, w_up, conv_w, conv_b.reshape(L, 1, F))


def _sconv_kernel(zx_ref, zb_ref, zc_ref, cw_ref, cb_ref, o_ref, *, n_ctx_tiles, seq_ctx, seq_lat):
    u = zc_ref[...] * zx_ref[...]
    pos, seq = _seq_pos(u.shape[0], n_ctx_tiles, seq_ctx, seq_lat)
    o_ref[...] = (zb_ref[...] * _dwconv3(u, pos, seq, cw_ref, cb_ref)).astype(o_ref.dtype)


def _sconv(z, col0, width, conv_w, conv_b, l, n_ctx_rows, seq_ctx, seq_lat, tm=1024, tn=512):
    M = z.shape[0]
    L = conv_w.shape[0]
    kern = functools.partial(_sconv_kernel, n_ctx_tiles=n_ctx_rows // tm, seq_ctx=seq_ctx, seq_lat=seq_lat)
    cb0 = col0 // tn
    wb = width // tn
    return pl.pallas_call(
        kern,
        grid=(M // tm, wb),
        in_specs=[
            pl.BlockSpec((tm, tn), lambda m, n: (m, cb0 + n)),
            pl.BlockSpec((tm, tn), lambda m, n: (m, cb0 + wb + n)),
            pl.BlockSpec((tm, tn), lambda m, n: (m, cb0 + 2 * wb + n)),
            pl.BlockSpec((None, 3, tn), lambda m, n: (l, 0, n)),
            pl.BlockSpec((None, 1, tn), lambda m, n: (l, 0, n)),
        ],
        out_specs=pl.BlockSpec((tm, tn), lambda m, n: (m, n)),
        out_shape=jax.ShapeDtypeStruct((M, width), BF16),
        compiler_params=_params("arbitrary", "arbitrary"),
        name="sconv",
    )(z, z, z, conv_w, conv_b.reshape(L, 1, width))


def _merge_kernel(a_ref, f_ref, c_ref, wa_ref, wf_ref, wc_ref, ga_ref, gf_ref, gc_ref, o_ref):
    wa = wa_ref[...].astype(BF16)
    wf = wf_ref[...].astype(BF16)
    wc = wc_ref[...].astype(BF16)
    for r0 in range(0, o_ref.shape[0], MERGE_CHUNK):
        rows = slice(r0, r0 + MERGE_CHUNK)

        def gated(x_ref, w, g_ref):
            return jax.nn.sigmoid(g_ref[rows, :].astype(F32)) * _bdot(x_ref[rows, :], w)

        o = gated(a_ref, wa, ga_ref) + gated(f_ref, wf, gf_ref) + gated(c_ref, wc, gc_ref)
        o_ref[rows, :] = o.astype(o_ref.dtype)


def _merge(att, fr, cv, w_na_out, w_fnet_out, w_conv_out, z, l, tm=1024, tn=512):
    M = att.shape[0]
    D = w_na_out.shape[2]
    gb = D // tn

    def act(a):
        return _act_spec(tm, a.shape[1], True)

    def wgt(w):
        return pl.BlockSpec((None, w.shape[1], tn), lambda m, n: (l, 0, n))

    def gate(j):
        return pl.BlockSpec((tm, tn), lambda m, n: (m, j * gb + n))

    return pl.pallas_call(
        _merge_kernel,
        grid=(M // tm, D // tn),
        in_specs=[act(att), act(fr), act(cv), wgt(w_na_out), wgt(w_fnet_out), wgt(w_conv_out),
                  gate(0), gate(1), gate(2)],
        out_specs=pl.BlockSpec((tm, tn), lambda m, n: (m, n)),
        out_shape=jax.ShapeDtypeStruct((M, D), BF16),
        compiler_params=_params("arbitrary", "arbitrary"),
        name="merge",
    )(att, fr, cv, w_na_out, w_fnet_out, w_conv_out, z, z, z)


def _softmax_pv(s_list, v_list):
    m = functools.reduce(jnp.maximum, [jnp.max(s, axis=-1, keepdims=True) for s in s_list])
    e_list = [jnp.exp(s - m) for s in s_list]
    denom = functools.reduce(jnp.add, [jnp.sum(e, axis=-1, keepdims=True) for e in e_list])
    outs = [_bdot(e.astype(BF16), v) for e, v in zip(e_list, v_list)]
    return functools.reduce(jnp.add, outs) * (1.0 / denom)


def _qk(q, k):
    return lax.dot_general(q, k, (((1,), (1,)), ((), ())), preferred_element_type=F32)


def _ctx_attn_kernel(q_ref, k_ref, v_ref, o_ref, *, scale):
    for h in range(N_HEADS):
        sl = slice(h * HEAD_DIM, (h + 1) * HEAD_DIM)
        q = q_ref[:, sl].astype(BF16)
        k = k_ref[:, sl].astype(BF16)
        v = v_ref[:, sl].astype(BF16)
        s = _qk(q, k) * scale
        o_ref[:, sl] = _softmax_pv([s], [v]).astype(o_ref.dtype)


def _ctx_attention(q, new_k, new_v, l, out_rows):
    n_seq, _, seq, W = new_k.shape
    kern = functools.partial(_ctx_attn_kernel, scale=HEAD_DIM ** -0.5)
    kv_spec = pl.BlockSpec((None, None, seq, W), lambda b: (b, l, 0, 0))
    return pl.pallas_call(
        kern,
        grid=(n_seq,),
        in_specs=[pl.BlockSpec((seq, W), lambda b: (b, 0)), kv_spec, kv_spec],
        out_specs=pl.BlockSpec((seq, W), lambda b: (b, 0)),
        out_shape=jax.ShapeDtypeStruct((out_rows, W), BF16),
        compiler_params=_params("arbitrary"),
        name="ctx_attention",
    )(q, new_k, new_v)


def _window_starts(rows):
    kr = min(NA_ROWS, rows)
    return [min(max(r - kr // 2, 0), rows - kr) for r in range(rows)], kr


def _row_groups(rows):
    starts, kr = _window_starts(rows)
    groups = []
    for r0 in range(0, rows, NA_GROUP_ROWS):
        r1 = min(r0 + NA_GROUP_ROWS, rows)
        groups.append((r0, r1, min(starts[r0:r1]), max(starts[r0:r1]) + kr))
    return groups


def _na_attn_kernel(att_ref, q_ref, k_ref, v_ref, kc_ref, vc_ref, *rest, scale, rows):
    bias_refs, o_ref = rest[:-1], rest[-1]
    q = q_ref[...].astype(BF16)
    k = k_ref[...].astype(BF16)
    v = v_ref[...].astype(BF16)
    vc = vc_ref[...].astype(BF16)
    s_ctx = _qk(q, kc_ref[...].astype(BF16)) * scale
    for (r0, r1, k0, k1), bias_ref in zip(_row_groups(rows), bias_refs):
        qs = slice(r0 * GRID_W, r1 * GRID_W)
        ks = slice(k0 * GRID_W, k1 * GRID_W)
        s_lat = _qk(q[qs], k[ks]) * scale + bias_ref[...]
        o_ref[qs, :] = _softmax_pv([s_lat, s_ctx[qs]], [v[ks], vc]).astype(o_ref.dtype)


def _na_attention(att, q, kv, row0, n_seq, seq, cache_k, cache_v, biases, l):
    P = cache_k.shape[2]
    r0 = row0 // seq
    rows = seq // GRID_W
    kern = functools.partial(_na_attn_kernel, scale=HEAD_DIM ** -0.5, rows=rows)

    def kvspec(j):
        return pl.BlockSpec((seq, HEAD_DIM), lambda h, b: (b, j * N_HEADS + h))

    cspec = pl.BlockSpec((None, None, P, HEAD_DIM), lambda h, b: (b, l, 0, h))
    bspecs = [pl.BlockSpec((None, None) + bias.shape[2:], lambda h, b: (l, h, 0, 0)) for bias in biases]
    return pl.pallas_call(
        kern,
        grid=(N_HEADS, n_seq),
        in_specs=[pl.BlockSpec(memory_space=pl.ANY),
                  pl.BlockSpec((seq, HEAD_DIM), lambda h, b: (r0 + b, h)),
                  kvspec(0), kvspec(1), cspec, cspec] + bspecs,
        out_specs=pl.BlockSpec((seq, HEAD_DIM), lambda h, b: (r0 + b, h)),
        out_shape=jax.ShapeDtypeStruct(att.shape, att.dtype),
        input_output_aliases={0: 0},
        compiler_params=_params("arbitrary", "arbitrary"),
        name="na_attention",
    )(att, q, kv, kv, cache_k, cache_v, *biases)


def _na_bias(rpb, rows):
    starts, kr = _window_starts(rows)
    col = np.arange(GRID_W)
    cs = np.clip(col - NA_COLS // 2, 0, GRID_W - NA_COLS)
    col_ok = (col[None, :] >= cs[:, None]) & (col[None, :] < cs[:, None] + NA_COLS)
    col_idx = np.clip(col[None, :] - col[:, None] + NA_COLS - 1, 0, 2 * NA_COLS - 2)
    onehot = (col_idx[None] == np.arange(2 * NA_COLS - 1)[:, None, None]) & col_ok[None]
    e = jnp.einsum('lhrd,dqk->lhqrk', rpb.astype(F32), jnp.asarray(onehot, F32),
                   precision=lax.Precision.HIGHEST)
    e = jnp.where(jnp.asarray(col_ok)[:, None, :], e, NEG_INF)
    biases = []
    for r0, r1, k0, k1 in _row_groups(rows):
        per_row = []
        for r in range(r0, r1):
            d0 = starts[r] - r + NA_ROWS - 1
            band = e[:, :, :, d0:d0 + kr]
            band = band.reshape(band.shape[:3] + (kr * GRID_W,))
            pad = ((starts[r] - k0) * GRID_W, (k1 - starts[r] - kr) * GRID_W)
            per_row.append(jnp.pad(band, ((0, 0), (0, 0), (0, 0), pad), constant_values=NEG_INF))
        biases.append(jnp.concatenate(per_row, axis=2))
    return biases


def _split_bf16(x):
    hi = x.astype(BF16)
    return hi, (x - hi.astype(F32)).astype(BF16)


def _dot_split(a, b):
    (ah, al), (bh, bl) = a, b
    return _bdot(ah, bh) + _bdot(ah, bl) + _bdot(al, bh)


def _fnet_kernel(x_ref, wch_ref, wcl_ref, wth_ref, wtl_ref, *rest):
    o_ref = rest[-1]
    t = _dot_split(_split_bf16(x_ref[...]), (wch_ref[...], wcl_ref[...]))
    g = t.shape[1] // 2
    u = jnp.concatenate([t[:, :g], t[:, g:]], axis=0)
    y = _dot_split((wth_ref[...], wtl_ref[...]), _split_bf16(u))
    o_ref[...] = y.astype(o_ref.dtype)


def _dft_mats(n):
    k = np.arange(n, dtype=np.int64)
    ang = 2.0 * np.pi * ((k[:, None] * k[None, :]) % n) / n
    s = 1.0 / np.sqrt(n)
    return np.cos(ang) * s, np.sin(ang) * s


def _split_const(w):
    w = jnp.asarray(w, F32)
    return _split_bf16(w)


def _fnet(z, col0, groups, gdim, row0, n_seq, seq, prev=None):
    ct, st = _dft_mats(seq)
    cc, sc = _dft_mats(gdim)
    wch, wcl = _split_const(np.concatenate([cc, sc], axis=1))
    wth, wtl = _split_const(np.concatenate([ct, -st], axis=1))
    r0 = row0 // seq
    c0 = col0 // gdim

    def const(w):
        return pl.BlockSpec(w.shape, lambda b, g: (0, 0))

    in_specs = [pl.BlockSpec((seq, gdim), lambda b, g: (r0 + b, c0 + g)),
                const(wch), const(wcl), const(wth), const(wtl)]
    args = [z, wch, wcl, wth, wtl]
    aliases = {}
    if prev is not None:
        in_specs.append(pl.BlockSpec(memory_space=pl.ANY))
        args.append(prev)
        aliases = {5: 0}
    return pl.pallas_call(
        _fnet_kernel,
        grid=(n_seq, groups),
        in_specs=in_specs,
        out_specs=pl.BlockSpec((seq, gdim), lambda b, g: (r0 + b, g)),
        out_shape=jax.ShapeDtypeStruct((z.shape[0], groups * gdim), BF16),
        input_output_aliases=aliases,
        compiler_params=_params("arbitrary", "arbitrary"),
        name="fnet",
    )(*args)


def kernel(x_prompt, x_sample, cache_k, cache_v, c, c_ctx, w_mod, b_mod, g_pre1, g_post1, g_pre2, g_post2,
           w_in, rpb, w_na_out, w_fnet_out, conv_w, conv_b, w_conv_out, w_o, w_up, ffn_conv_w, ffn_conv_b,
           w_down):
    B, S, D = x_prompt.shape
    Bd, T, _ = x_sample.shape
    L = w_mod.shape[0]
    P = cache_k.shape[2]
    W = N_HEADS * HEAD_DIM
    n_ctx = B * S
    n_lat = Bd * T
    fw = w_fnet_out.shape[1]
    cwid = w_conv_out.shape[1]
    fgroups = 4
    assert T == ROW_GROUP and n_ctx % ROW_GROUP == 0 and Bd + 1 <= 8

    x = jnp.concatenate([x_prompt.reshape(n_ctx, D), x_sample.reshape(n_lat, D)], axis=0)

    c_rows = jnp.zeros((8, D), F32).at[:Bd].set(c).at[Bd].set(c_ctx)
    mod = _modulation(c_rows, w_mod, b_mod)
    group_row = np.concatenate([np.full(n_ctx // ROW_GROUP, Bd), np.arange(Bd)])
    mods = mod.reshape(L, 8, 6, D)[:, group_row]
    mods = mods.transpose(0, 2, 1, 3)[:, :, :, None, :]

    gains = [g.reshape(L, 1, D) for g in (g_pre1, g_post1, g_pre2, g_post2)]
    ck = cache_k.reshape(Bd, L, P, W)
    cv_cache = cache_v.reshape(Bd, L, P, W)

    col_f = 3 * W
    col_gate = col_f + fw + 3 * cwid
    biases = _na_bias(rpb, T // GRID_W)

    new_kv = None
    h = _prenorm(x, gains[0], mods, 0, 1, 0)
    for l in range(L):
        tm, tn = PROJ_TM, PROJ_TN
        q = _matmul(h, w_in, l, BF16, tm, tn, "proj_q", 0, W)
        new_kv = _cache_proj(h, w_in, l, W, 2 * W, new_kv, B, S, L, tm, tn // 2)
        new_k, new_v = new_kv
        kv_lat = _matmul(h, w_in, l, BF16, tm, tn, "proj_kv_lat", W, 2 * W, row0=n_ctx)
        fxbc = _matmul(h, w_in, l, F32, tm, tn, "proj_fxbc", col_f, col_gate - col_f)
        zg = _matmul(h, w_in, l, BF16, tm, tn, "proj_gates", col_gate)
        att = _ctx_attention(q, new_k, new_v, l, n_ctx + n_lat)
        att = _na_attention(att, q, kv_lat, n_ctx, Bd, T, ck, cv_cache, biases, l)
        fr = _fnet(fxbc, 0, fgroups, fw // fgroups, 0, B, S)
        fr = _fnet(fxbc, 0, fgroups, fw // fgroups, n_ctx, Bd, T, prev=fr)
        cv = _sconv(fxbc, fw, cwid, conv_w, conv_b, l, n_ctx, S, T)
        merged = _merge(att, fr, cv, w_na_out, w_fnet_out, w_conv_out, zg, l)
        y = _matmul(merged, w_o, l, F32, tm, tn, "proj_o")
        x, h = _resid(x, y, gains[1], mods, l, 2, (gains[2], l, 4, 3))
        act = _ffn_up(h, w_up, ffn_conv_w, ffn_conv_b, l, n_ctx, S, T)
        y = _matmul(act, w_down, l, F32, 1024, 256, "ffn_down", single_buffer=True)
        if l + 1 < L:
            x, h = _resid(x, y, gains[3], mods, l, 5, (gains[0], l + 1, 1, 0))

    y_prompt = _resid_last(x, y, gains[3], mods, L - 1, 5, 0, n_ctx).reshape(B, S, D)
    y_sample = _resid_last(x, y, gains[3], mods, L - 1, 5, n_ctx, n_lat).reshape(Bd, T, D)
    kv_shape = (B, L, S, N_HEADS, HEAD_DIM)
    return y_prompt, y_sample, new_k.reshape(kv_shape), new_v.reshape(kv_shape)
```

```python
import functools

import numpy as np
import jax
import jax.numpy as jnp
from jax import lax
from jax.experimental import pallas as pl
from jax.experimental.pallas import tpu as pltpu

F32 = jnp.float32
BF16 = jnp.bfloat16

EPS = 1e-6
NEG_INF = -1e30
GRID_W = 64
NA_ROWS = 8
NA_COLS = 16
N_HEADS = 16
HEAD_DIM = 128
MERGE_CHUNK = 256
FFN_SPLITS = (0.75,)
NA_GROUP_ROWS = 4

VMEM_LIMIT_BYTES = 58 * 1024 * 1024
ROW_GROUP = 1024
PROJ_TM, PROJ_TN = 1024, 512


def _params(*sem):
    return pltpu.CompilerParams(dimension_semantics=sem, vmem_limit_bytes=VMEM_LIMIT_BYTES)


def _bdot(a, b):
    return jnp.dot(a, b, preferred_element_type=F32)


def _mod_kernel(c_ref, w_ref, b_ref, o_ref):
    c = c_ref[...]
    s = (c * jax.nn.sigmoid(c)).astype(BF16)
    o_ref[...] = _bdot(s, w_ref[...].astype(BF16)) + b_ref[...]


def _modulation(c_rows, w_mod, b_mod, tn=512):
    L, D, N = w_mod.shape
    return pl.pallas_call(
        _mod_kernel,
        grid=(L, N // tn),
        in_specs=[
            pl.BlockSpec((8, D), lambda l, n: (0, 0)),
            pl.BlockSpec((None, D, tn), lambda l, n: (l, 0, n)),
            pl.BlockSpec((None, 1, tn), lambda l, n: (l, 0, n)),
        ],
        out_specs=pl.BlockSpec((None, 8, tn), lambda l, n: (l, 0, n)),
        out_shape=jax.ShapeDtypeStruct((L, 8, N), F32),
        compiler_params=_params("arbitrary", "arbitrary"),
        name="modulation",
    )(c_rows, w_mod, b_mod.reshape(L, 1, N))


def _rms(x):
    return x * lax.rsqrt(jnp.mean(x * x, axis=-1, keepdims=True) + EPS)


def _prenorm_kernel(x_ref, g_ref, sc_ref, sh_ref, h_ref):
    h = _rms(x_ref[...]) * g_ref[...]
    h_ref[...] = (h * (1.0 + sc_ref[...]) + sh_ref[...]).astype(BF16)


def _resid_kernel(x_ref, y_ref, gpost_ref, gate_ref, gpre_ref, sc_ref, sh_ref, xo_ref, h_ref):
    x = x_ref[...] + gate_ref[...] * (_rms(y_ref[...]) * gpost_ref[...])
    xo_ref[...] = x
    h = _rms(x) * gpre_ref[...]
    h_ref[...] = (h * (1.0 + sc_ref[...]) + sh_ref[...]).astype(BF16)


def _resid_last_kernel(x_ref, y_ref, gpost_ref, gate_ref, xo_ref):
    xo_ref[...] = x_ref[...] + gate_ref[...] * (_rms(y_ref[...]) * gpost_ref[...])


def _row_spec(tr, D, i0=0):
    return pl.BlockSpec((tr, D), lambda i: (i0 + i, 0))


def _gain_spec(l, D):
    return pl.BlockSpec((None, 1, D), lambda i: (l, 0, 0))


def _mod_spec(l, j, tr, D, i0=0):
    return pl.BlockSpec((None, None, None, 1, D), lambda i: (l, j, ((i0 + i) * tr) // ROW_GROUP, 0, 0))


def _prenorm(x, gains, mods, l, j_sc, j_sh, tr=256):
    M, D = x.shape
    return pl.pallas_call(
        _prenorm_kernel,
        grid=(M // tr,),
        in_specs=[_row_spec(tr, D), _gain_spec(l, D), _mod_spec(l, j_sc, tr, D), _mod_spec(l, j_sh, tr, D)],
        out_specs=_row_spec(tr, D),
        out_shape=jax.ShapeDtypeStruct((M, D), BF16),
        compiler_params=_params("arbitrary"),
        name="prenorm",
    )(x, gains, mods, mods)


def _resid_last(x, y, gpost, mods, l, j_gate, row0, nrows, tr=256):
    D = x.shape[1]
    i0 = row0 // tr
    return pl.pallas_call(
        _resid_last_kernel,
        grid=(nrows // tr,),
        in_specs=[_row_spec(tr, D, i0), _row_spec(tr, D, i0), _gain_spec(l, D),
                  _mod_spec(l, j_gate, tr, D, i0)],
        out_specs=_row_spec(tr, D),
        out_shape=jax.ShapeDtypeStruct((nrows, D), F32),
        compiler_params=_params("arbitrary"),
        name="resid_last",
    )(x, y, gpost, mods)


def _resid(x, y, gpost, mods, l, j_gate, nxt, tr=256):
    M, D = x.shape
    ins = [_row_spec(tr, D), _row_spec(tr, D), _gain_spec(l, D), _mod_spec(l, j_gate, tr, D)]
    args = [x, y, gpost, mods]
    gpre, ln, j_sc, j_sh = nxt
    ins += [_gain_spec(ln, D), _mod_spec(ln, j_sc, tr, D), _mod_spec(ln, j_sh, tr, D)]
    args += [gpre, mods, mods]
    return pl.pallas_call(
        _resid_kernel,
        grid=(M // tr,),
        in_specs=ins,
        out_specs=[_row_spec(tr, D), _row_spec(tr, D)],
        out_shape=[jax.ShapeDtypeStruct((M, D), F32), jax.ShapeDtypeStruct((M, D), BF16)],
        compiler_params=_params("arbitrary"),
        name="resid",
    )(*args)


def _mm_kernel(x_ref, w_ref, o_ref):
    o_ref[...] = _bdot(x_ref[...], w_ref[...].astype(BF16)).astype(o_ref.dtype)


def _act_spec(tm, K, single_buffer, m0=0):
    if single_buffer:
        return pl.BlockSpec((tm, K), lambda m, n: (m0 + m, 0), pipeline_mode=pl.Buffered(1))
    return pl.BlockSpec((tm, K), lambda m, n: (m0 + m, 0))


def _matmul(x, w, l, out_dtype, tm, tn, name, col0=0, ncols=None, row0=0, nrows=None, single_buffer=False):
    K = x.shape[1]
    M = x.shape[0] - row0 if nrows is None else nrows
    N = w.shape[2] - col0 if ncols is None else ncols
    c0 = col0 // tn
    return pl.pallas_call(
        _mm_kernel,
        grid=(M // tm, N // tn),
        in_specs=[
            _act_spec(tm, K, single_buffer, row0 // tm),
            pl.BlockSpec((None, K, tn), lambda m, n: (l, 0, c0 + n)),
        ],
        out_specs=pl.BlockSpec((tm, tn), lambda m, n: (m, n)),
        out_shape=jax.ShapeDtypeStruct((M, N), out_dtype),
        compiler_params=_params("arbitrary", "arbitrary"),
        name=name,
    )(x, w)


def _proj_in_kernel(x_ref, w_ref, *rest, nq, nf, n_ctx_tiles):
    q_ref, k_ref, v_ref, kv_ref, f_ref, g_ref = rest[-6:]
    m, n = pl.program_id(0), pl.program_id(1)
    is_ctx = m < n_ctx_tiles

    def proj():
        return _bdot(x_ref[...], w_ref[...].astype(BF16))

    @pl.when(n < nq)
    def _():
        q_ref[...] = proj().astype(q_ref.dtype)

    @pl.when((n >= nq) & (n < 2 * nq) & is_ctx)
    def _():
        k_ref[...] = proj().reshape(k_ref.shape)

    @pl.when((n >= 2 * nq) & (n < 3 * nq) & is_ctx)
    def _():
        v_ref[...] = proj().reshape(v_ref.shape)

    @pl.when((n >= nq) & (n < 3 * nq) & jnp.logical_not(is_ctx))
    def _():
        kv_ref[...] = proj().astype(kv_ref.dtype)

    @pl.when((n >= 3 * nq) & (n < 3 * nq + nf))
    def _():
        f_ref[...] = proj()

    @pl.when(n >= 3 * nq + nf)
    def _():
        g_ref[...] = proj().astype(g_ref.dtype)


def _proj_in(h, w, l, caches, n_seq, seq, depth, n_f, tm, tn):
    M, K = h.shape
    N = w.shape[2]
    W = N_HEADS * HEAD_DIM
    nq, nf = W // tn, n_f // tn
    ng = N // tn - 3 * nq - nf
    mc = n_seq * seq // tm
    spt = tm // seq

    def cache_map(first):
        def index(m, n):
            ctx = m < mc
            return (jnp.where(ctx, m, mc - 1), l, 0, jnp.where(ctx, jnp.clip(n - first, 0, nq - 1), nq - 1))
        return index

    def kv_lat_map(m, n):
        return (jnp.maximum(m - mc, 0), jnp.where(m >= mc, jnp.clip(n - nq, 0, 2 * nq - 1), 0))

    in_specs = [
        pl.BlockSpec((tm, K), lambda m, n: (m, 0), pipeline_mode=pl.Buffered(1)),
        pl.BlockSpec((None, K, tn), lambda m, n: (l, 0, n)),
    ]
    args = [h, w]
    aliases = {}
    if caches is not None:
        in_specs += [pl.BlockSpec(memory_space=pl.ANY)] * 2
        args += list(caches)
        aliases = {2: 1, 3: 2}
    cache_shape = jax.ShapeDtypeStruct((n_seq, depth, seq, W), F32)
    kern = functools.partial(_proj_in_kernel, nq=nq, nf=nf, n_ctx_tiles=mc)
    q, new_k, new_v, kv_lat, fxbc, zg = pl.pallas_call(
        kern,
        grid=(M // tm, N // tn),
        in_specs=in_specs,
        out_specs=[
            pl.BlockSpec((tm, tn), lambda m, n: (m, jnp.clip(n, 0, nq - 1))),
            pl.BlockSpec((spt, None, seq, tn), cache_map(nq)),
            pl.BlockSpec((spt, None, seq, tn), cache_map(2 * nq)),
            pl.BlockSpec((tm, tn), kv_lat_map),
            pl.BlockSpec((tm, tn), lambda m, n: (m, jnp.clip(n - 3 * nq, 0, nf - 1))),
            pl.BlockSpec((tm, tn), lambda m, n: (m, jnp.clip(n - 3 * nq - nf, 0, ng - 1))),
        ],
        out_shape=[
            jax.ShapeDtypeStruct((M, W), BF16), cache_shape, cache_shape,
            jax.ShapeDtypeStruct((M - mc * tm, 2 * W), BF16),
            jax.ShapeDtypeStruct((M, n_f), F32),
            jax.ShapeDtypeStruct((M, ng * tn), BF16),
        ],
        input_output_aliases=aliases,
        compiler_params=_params("arbitrary", "arbitrary"),
        name="proj_in",
    )(*args)
    return q, (new_k, new_v), kv_lat, fxbc, zg


def _seq_pos(tm, n_ctx_tiles, seq_ctx, seq_lat):
    seq = jnp.where(pl.program_id(0) < n_ctx_tiles, seq_ctx, seq_lat)
    row = lax.broadcasted_iota(jnp.int32, (tm, 1), 0)
    return row & (seq - 1), seq


def _dwconv3(u, pos, seq, w_ref, b_ref):
    tm = u.shape[0]
    prev = jnp.where(pos == 0, 0.0, pltpu.roll(u, 1, 0))
    nxt = jnp.where(pos == seq - 1, 0.0, pltpu.roll(u, tm - 1, 0))
    return prev * w_ref[0:1, :] + u * w_ref[1:2, :] + nxt * w_ref[2:3, :] + b_ref[...]


def _ffn_up_kernel(h_ref, wg_ref, wv_ref, cw_ref, cb_ref, o_ref, *, n_ctx_tiles, seq_ctx, seq_lat):
    tm = h_ref.shape[0]
    wg = wg_ref[...].astype(BF16)
    wv = wv_ref[...].astype(BF16)
    seq = jnp.where(pl.program_id(0) < n_ctx_tiles, seq_ctx, seq_lat)
    bounds = [0] + [int(tm * f) for f in FFN_SPLITS] + [tm]
    gate = jnp.concatenate([_bdot(h_ref[r0:r1, :], wg) for r0, r1 in zip(bounds[:-1], bounds[1:])], axis=0)
    for r0, r1 in zip(bounds[:-1], bounds[1:]):
        val = _bdot(h_ref[r0:r1, :], wv)
        lo, hi = max(r0 - 8, 0), min(r1 + 8, tm)
        pos = (lo + lax.broadcasted_iota(jnp.int32, (hi - lo, 1), 0)) & (seq - 1)
        g = _dwconv3(gate[lo:hi], pos, seq, cw_ref, cb_ref)[r0 - lo:r1 - lo]
        o_ref[r0:r1, :] = (g * jax.nn.sigmoid(g) * val).astype(o_ref.dtype)


def _ffn_up(h, w_up, conv_w, conv_b, l, n_ctx_rows, seq_ctx, seq_lat, tm=1024, tn=256):
    M, K = h.shape
    L, _, N2 = w_up.shape
    F = N2 // 2
    nb = F // tn
    kern = functools.partial(_ffn_up_kernel, n_ctx_tiles=n_ctx_rows // tm, seq_ctx=seq_ctx, seq_lat=seq_lat)
    return pl.pallas_call(
        kern,
        grid=(M // tm, nb),
        in_specs=[
            pl.BlockSpec((tm, K), lambda m, n: (m, 0)),
            pl.BlockSpec((None, K, tn), lambda m, n: (l, 0, n)),
            pl.BlockSpec((None, K, tn), lambda m, n: (l, 0, nb + n)),
            pl.BlockSpec((None, 3, tn), lambda m, n: (l, 0, n)),
            pl.BlockSpec((None, 1, tn), lambda m, n: (l, 0, n)),
        ],
        out_specs=pl.BlockSpec((tm, tn), lambda m, n: (m, n)),
        out_shape=jax.ShapeDtypeStruct((M, F), BF16),
        compiler_params=_params("arbitrary", "arbitrary"),
        name="ffn_up",
    )(h, w_up, w_up, conv_w, conv_b.reshape(L, 1, F))


def _sconv_kernel(zx_ref, zb_ref, zc_ref, cw_ref, cb_ref, o_ref, *, n_ctx_tiles, seq_ctx, seq_lat):
    u = zc_ref[...] * zx_ref[...]
    pos, seq = _seq_pos(u.shape[0], n_ctx_tiles, seq_ctx, seq_lat)
    o_ref[...] = (zb_ref[...] * _dwconv3(u, pos, seq, cw_ref, cb_ref)).astype(o_ref.dtype)


def _sconv(z, col0, width, conv_w, conv_b, l, n_ctx_rows, seq_ctx, seq_lat, tm=1024, tn=512):
    M = z.shape[0]
    L = conv_w.shape[0]
    kern = functools.partial(_sconv_kernel, n_ctx_tiles=n_ctx_rows // tm, seq_ctx=seq_ctx, seq_lat=seq_lat)
    cb0 = col0 // tn
    wb = width // tn
    return pl.pallas_call(
        kern,
        grid=(M // tm, wb),
        in_specs=[
            pl.BlockSpec((tm, tn), lambda m, n: (m, cb0 + n)),
            pl.BlockSpec((tm, tn), lambda m, n: (m, cb0 + wb + n)),
            pl.BlockSpec((tm, tn), lambda m, n: (m, cb0 + 2 * wb + n)),
            pl.BlockSpec((None, 3, tn), lambda m, n: (l, 0, n)),
            pl.BlockSpec((None, 1, tn), lambda m, n: (l, 0, n)),
        ],
        out_specs=pl.BlockSpec((tm, tn), lambda m, n: (m, n)),
        out_shape=jax.ShapeDtypeStruct((M, width), BF16),
        compiler_params=_params("arbitrary", "arbitrary"),
        name="sconv",
    )(z, z, z, conv_w, conv_b.reshape(L, 1, width))


def _merge_kernel(a_ref, f_ref, c_ref, wa_ref, wf_ref, wc_ref, ga_ref, gf_ref, gc_ref, o_ref):
    wa = wa_ref[...].astype(BF16)
    wf = wf_ref[...].astype(BF16)
    wc = wc_ref[...].astype(BF16)
    for r0 in range(0, o_ref.shape[0], MERGE_CHUNK):
        rows = slice(r0, r0 + MERGE_CHUNK)

        def gated(x_ref, w, g_ref):
            return jax.nn.sigmoid(g_ref[rows, :].astype(F32)) * _bdot(x_ref[rows, :], w)

        o = gated(a_ref, wa, ga_ref) + gated(f_ref, wf, gf_ref) + gated(c_ref, wc, gc_ref)
        o_ref[rows, :] = o.astype(o_ref.dtype)


def _merge(att, fr, cv, w_na_out, w_fnet_out, w_conv_out, z, l, tm=1024, tn=512):
    M = att.shape[0]
    D = w_na_out.shape[2]
    gb = D // tn

    def act(a):
        return _act_spec(tm, a.shape[1], True)

    def wgt(w):
        return pl.BlockSpec((None, w.shape[1], tn), lambda m, n: (l, 0, n))

    def gate(j):
        return pl.BlockSpec((tm, tn), lambda m, n: (m, j * gb + n))

    return pl.pallas_call(
        _merge_kernel,
        grid=(M // tm, D // tn),
        in_specs=[act(att), act(fr), act(cv), wgt(w_na_out), wgt(w_fnet_out), wgt(w_conv_out),
                  gate(0), gate(1), gate(2)],
        out_specs=pl.BlockSpec((tm, tn), lambda m, n: (m, n)),
        out_shape=jax.ShapeDtypeStruct((M, D), BF16),
        compiler_params=_params("arbitrary", "arbitrary"),
        name="merge",
    )(att, fr, cv, w_na_out, w_fnet_out, w_conv_out, z, z, z)


def _softmax_pv(s_list, v_list):
    m = functools.reduce(jnp.maximum, [jnp.max(s, axis=-1, keepdims=True) for s in s_list])
    e_list = [jnp.exp(s - m) for s in s_list]
    denom = functools.reduce(jnp.add, [jnp.sum(e, axis=-1, keepdims=True) for e in e_list])
    outs = [_bdot(e.astype(BF16), v) for e, v in zip(e_list, v_list)]
    return functools.reduce(jnp.add, outs) * (1.0 / denom)


def _qk(q, k):
    return lax.dot_general(q, k, (((1,), (1,)), ((), ())), preferred_element_type=F32)


def _ctx_attn_kernel(q_ref, k_ref, v_ref, o_ref, *, scale):
    for h in range(N_HEADS):
        sl = slice(h * HEAD_DIM, (h + 1) * HEAD_DIM)
        q = q_ref[:, sl].astype(BF16)
        k = k_ref[:, sl].astype(BF16)
        v = v_ref[:, sl].astype(BF16)
        s = _qk(q, k) * scale
        o_ref[:, sl] = _softmax_pv([s], [v]).astype(o_ref.dtype)


def _ctx_attention(q, new_k, new_v, l, out_rows):
    n_seq, _, seq, W = new_k.shape
    kern = functools.partial(_ctx_attn_kernel, scale=HEAD_DIM ** -0.5)
    kv_spec = pl.BlockSpec((None, None, seq, W), lambda b: (b, l, 0, 0))
    return pl.pallas_call(
        kern,
        grid=(n_seq,),
        in_specs=[pl.BlockSpec((seq, W), lambda b: (b, 0)), kv_spec, kv_spec],
        out_specs=pl.BlockSpec((seq, W), lambda b: (b, 0)),
        out_shape=jax.ShapeDtypeStruct((out_rows, W), BF16),
        compiler_params=_params("arbitrary"),
        name="ctx_attention",
    )(q, new_k, new_v)


def _window_starts(rows):
    kr = min(NA_ROWS, rows)
    return [min(max(r - kr // 2, 0), rows - kr) for r in range(rows)], kr


def _row_groups(rows):
    starts, kr = _window_starts(rows)
    groups = []
    for r0 in range(0, rows, NA_GROUP_ROWS):
        r1 = min(r0 + NA_GROUP_ROWS, rows)
        groups.append((r0, r1, min(starts[r0:r1]), max(starts[r0:r1]) + kr))
    return groups


def _na_attn_kernel(att_ref, q_ref, k_ref, v_ref, kc_ref, vc_ref, *rest, scale, rows):
    bias_refs, o_ref = rest[:-1], rest[-1]
    q = q_ref[...].astype(BF16)
    k = k_ref[...].astype(BF16)
    v = v_ref[...].astype(BF16)
    vc = vc_ref[...].astype(BF16)
    s_ctx = _qk(q, kc_ref[...].astype(BF16)) * scale
    for (r0, r1, k0, k1), bias_ref in zip(_row_groups(rows), bias_refs):
        qs = slice(r0 * GRID_W, r1 * GRID_W)
        ks = slice(k0 * GRID_W, k1 * GRID_W)
        s_lat = _qk(q[qs], k[ks]) * scale + bias_ref[...]
        o_ref[qs, :] = _softmax_pv([s_lat, s_ctx[qs]], [v[ks], vc]).astype(o_ref.dtype)


def _na_attention(att, q, kv, row0, n_seq, seq, cache_k, cache_v, biases, l):
    P = cache_k.shape[2]
    r0 = row0 // seq
    rows = seq // GRID_W
    kern = functools.partial(_na_attn_kernel, scale=HEAD_DIM ** -0.5, rows=rows)

    def kvspec(j):
        return pl.BlockSpec((seq, HEAD_DIM), lambda h, b: (b, j * N_HEADS + h))

    cspec = pl.BlockSpec((None, None, P, HEAD_DIM), lambda h, b: (b, l, 0, h))
    bspecs = [pl.BlockSpec((None, None) + bias.shape[2:], lambda h, b: (l, h, 0, 0)) for bias in biases]
    return pl.pallas_call(
        kern,
        grid=(N_HEADS, n_seq),
        in_specs=[pl.BlockSpec(memory_space=pl.ANY),
                  pl.BlockSpec((seq, HEAD_DIM), lambda h, b: (r0 + b, h)),
                  kvspec(0), kvspec(1), cspec, cspec] + bspecs,
        out_specs=pl.BlockSpec((seq, HEAD_DIM), lambda h, b: (r0 + b, h)),
        out_shape=jax.ShapeDtypeStruct(att.shape, att.dtype),
        input_output_aliases={0: 0},
        compiler_params=_params("arbitrary", "arbitrary"),
        name="na_attention",
    )(att, q, kv, kv, cache_k, cache_v, *biases)


def _na_bias(rpb, rows):
    starts, kr = _window_starts(rows)
    col = np.arange(GRID_W)
    cs = np.clip(col - NA_COLS // 2, 0, GRID_W - NA_COLS)
    col_ok = (col[None, :] >= cs[:, None]) & (col[None, :] < cs[:, None] + NA_COLS)
    col_idx = np.clip(col[None, :] - col[:, None] + NA_COLS - 1, 0, 2 * NA_COLS - 2)
    onehot = (col_idx[None] == np.arange(2 * NA_COLS - 1)[:, None, None]) & col_ok[None]
    e = jnp.einsum('lhrd,dqk->lhqrk', rpb.astype(F32), jnp.asarray(onehot, F32),
                   precision=lax.Precision.HIGHEST)
    e = jnp.where(jnp.asarray(col_ok)[:, None, :], e, NEG_INF)
    biases = []
    for r0, r1, k0, k1 in _row_groups(rows):
        per_row = []
        for r in range(r0, r1):
            d0 = starts[r] - r + NA_ROWS - 1
            band = e[:, :, :, d0:d0 + kr]
            band = band.reshape(band.shape[:3] + (kr * GRID_W,))
            pad = ((starts[r] - k0) * GRID_W, (k1 - starts[r] - kr) * GRID_W)
            per_row.append(jnp.pad(band, ((0, 0), (0, 0), (0, 0), pad), constant_values=NEG_INF))
        biases.append(jnp.concatenate(per_row, axis=2))
    return biases


def _split_bf16(x):
    hi = x.astype(BF16)
    return hi, (x - hi.astype(F32)).astype(BF16)


def _dot_split(a, b):
    (ah, al), (bh, bl) = a, b
    return _bdot(ah, bh) + _bdot(ah, bl) + _bdot(al, bh)


def _fnet_kernel(x_ref, wch_ref, wcl_ref, wth_ref, wtl_ref, *rest):
    o_ref = rest[-1]
    t = _dot_split(_split_bf16(x_ref[...]), (wch_ref[...], wcl_ref[...]))
    g = t.shape[1] // 2
    u = jnp.concatenate([t[:, :g], t[:, g:]], axis=0)
    y = _dot_split((wth_ref[...], wtl_ref[...]), _split_bf16(u))
    o_ref[...] = y.astype(o_ref.dtype)


def _dft_mats(n):
    k = np.arange(n, dtype=np.int64)
    ang = 2.0 * np.pi * ((k[:, None] * k[None, :]) % n) / n
    s = 1.0 / np.sqrt(n)
    return np.cos(ang) * s, np.sin(ang) * s


def _split_const(w):
    w = jnp.asarray(w, F32)
    return _split_bf16(w)


def _fnet(z, col0, groups, gdim, row0, n_seq, seq, prev=None):
    ct, st = _dft_mats(seq)
    cc, sc = _dft_mats(gdim)
    wch, wcl = _split_const(np.concatenate([cc, sc], axis=1))
    wth, wtl = _split_const(np.concatenate([ct, -st], axis=1))
    r0 = row0 // seq
    c0 = col0 // gdim

    def const(w):
        return pl.BlockSpec(w.shape, lambda b, g: (0, 0))

    in_specs = [pl.BlockSpec((seq, gdim), lambda b, g: (r0 + b, c0 + g)),
                const(wch), const(wcl), const(wth), const(wtl)]
    args = [z, wch, wcl, wth, wtl]
    aliases = {}
    if prev is not None:
        in_specs.append(pl.BlockSpec(memory_space=pl.ANY))
        args.append(prev)
        aliases = {5: 0}
    return pl.pallas_call(
        _fnet_kernel,
        grid=(n_seq, groups),
        in_specs=in_specs,
        out_specs=pl.BlockSpec((seq, gdim), lambda b, g: (r0 + b, g)),
        out_shape=jax.ShapeDtypeStruct((z.shape[0], groups * gdim), BF16),
        input_output_aliases=aliases,
        compiler_params=_params("arbitrary", "arbitrary"),
        name="fnet",
    )(*args)


def kernel(x_prompt, x_sample, cache_k, cache_v, c, c_ctx, w_mod, b_mod, g_pre1, g_post1, g_pre2, g_post2,
           w_in, rpb, w_na_out, w_fnet_out, conv_w, conv_b, w_conv_out, w_o, w_up, ffn_conv_w, ffn_conv_b,
           w_down):
    B, S, D = x_prompt.shape
    Bd, T, _ = x_sample.shape
    L = w_mod.shape[0]
    P = cache_k.shape[2]
    W = N_HEADS * HEAD_DIM
    n_ctx = B * S
    n_lat = Bd * T
    fw = w_fnet_out.shape[1]
    cwid = w_conv_out.shape[1]
    fgroups = 4
    assert T == ROW_GROUP and n_ctx % ROW_GROUP == 0 and Bd + 1 <= 8

    x = jnp.concatenate([x_prompt.reshape(n_ctx, D), x_sample.reshape(n_lat, D)], axis=0)

    c_rows = jnp.zeros((8, D), F32).at[:Bd].set(c).at[Bd].set(c_ctx)
    mod = _modulation(c_rows, w_mod, b_mod)
    group_row = np.concatenate([np.full(n_ctx // ROW_GROUP, Bd), np.arange(Bd)])
    mods = mod.reshape(L, 8, 6, D)[:, group_row]
    mods = mods.transpose(0, 2, 1, 3)[:, :, :, None, :]

    gains = [g.reshape(L, 1, D) for g in (g_pre1, g_post1, g_pre2, g_post2)]
    ck = cache_k.reshape(Bd, L, P, W)
    cv_cache = cache_v.reshape(Bd, L, P, W)

    col_f = 3 * W
    col_gate = col_f + fw + 3 * cwid
    biases = _na_bias(rpb, T // GRID_W)

    new_kv = None
    h = _prenorm(x, gains[0], mods, 0, 1, 0)
    for l in range(L):
        tm, tn = PROJ_TM, PROJ_TN
        q, new_kv, kv_lat, fxbc, zg = _proj_in(h, w_in, l, new_kv, B, S, L, col_gate - col_f, tm, tn)
        new_k, new_v = new_kv
        att = _ctx_attention(q, new_k, new_v, l, n_ctx + n_lat)
        att = _na_attention(att, q, kv_lat, n_ctx, Bd, T, ck, cv_cache, biases, l)
        fr = _fnet(fxbc, 0, fgroups, fw // fgroups, 0, B, S)
        fr = _fnet(fxbc, 0, fgroups, fw // fgroups, n_ctx, Bd, T, prev=fr)
        cv = _sconv(fxbc, fw, cwid, conv_w, conv_b, l, n_ctx, S, T)
        merged = _merge(att, fr, cv, w_na_out, w_fnet_out, w_conv_out, zg, l)
        y = _matmul(merged, w_o, l, F32, tm, tn, "proj_o")
        x, h = _resid(x, y, gains[1], mods, l, 2, (gains[2], l, 4, 3))
        act = _ffn_up(h, w_up, ffn_conv_w, ffn_conv_b, l, n_ctx, S, T)
        y = _matmul(act, w_down, l, F32, 1024, 256, "ffn_down", single_buffer=True)
        if l + 1 < L:
            x, h = _resid(x, y, gains[3], mods, l, 5, (gains[0], l + 1, 1, 0))

    y_prompt = _resid_last(x, y, gains[3], mods, L - 1, 5, 0, n_ctx).reshape(B, S, D)
    y_sample = _resid_last(x, y, gains[3], mods, L - 1, 5, n_ctx, n_lat).reshape(Bd, T, D)
    kv_shape = (B, L, S, N_HEADS, HEAD_DIM)
    return y_prompt, y_sample, new_k.reshape(kv_shape), new_v.reshape(kv_shape)
```

```python
import functools

import numpy as np
import jax
import jax.numpy as jnp
from jax import lax
from jax.experimental import pallas as pl
from jax.experimental.pallas import tpu as pltpu

F32 = jnp.float32
BF16 = jnp.bfloat16

EPS = 1e-6
NEG_INF = -1e30
GRID_W = 64
NA_ROWS = 8
NA_COLS = 16
N_HEADS = 16
HEAD_DIM = 128
MERGE_CHUNK = 256
FFN_SPLITS = (0.75,)
NA_GROUP_ROWS = 4

VMEM_LIMIT_BYTES = 58 * 1024 * 1024
ROW_GROUP = 1024
PROJ_TM, PROJ_TN = 2048, 256
MM_ROWS = 1024


def _params(*sem):
    return pltpu.CompilerParams(dimension_semantics=sem, vmem_limit_bytes=VMEM_LIMIT_BYTES)


def _bdot(a, b):
    return jnp.dot(a, b, preferred_element_type=F32)


def _mod_kernel(c_ref, w_ref, b_ref, o_ref):
    c = c_ref[...]
    s = (c * jax.nn.sigmoid(c)).astype(BF16)
    o_ref[...] = _bdot(s, w_ref[...].astype(BF16)) + b_ref[...]


def _modulation(c_rows, w_mod, b_mod, tn=512):
    L, D, N = w_mod.shape
    return pl.pallas_call(
        _mod_kernel,
        grid=(L, N // tn),
        in_specs=[
            pl.BlockSpec((8, D), lambda l, n: (0, 0)),
            pl.BlockSpec((None, D, tn), lambda l, n: (l, 0, n)),
            pl.BlockSpec((None, 1, tn), lambda l, n: (l, 0, n)),
        ],
        out_specs=pl.BlockSpec((None, 8, tn), lambda l, n: (l, 0, n)),
        out_shape=jax.ShapeDtypeStruct((L, 8, N), F32),
        compiler_params=_params("arbitrary", "arbitrary"),
        name="modulation",
    )(c_rows, w_mod, b_mod.reshape(L, 1, N))


def _rms(x):
    return x * lax.rsqrt(jnp.mean(x * x, axis=-1, keepdims=True) + EPS)


def _prenorm_kernel(x_ref, g_ref, sc_ref, sh_ref, h_ref):
    h = _rms(x_ref[...]) * g_ref[...]
    h_ref[...] = (h * (1.0 + sc_ref[...]) + sh_ref[...]).astype(BF16)


def _resid_kernel(x_ref, y_ref, gpost_ref, gate_ref, gpre_ref, sc_ref, sh_ref, xo_ref, h_ref):
    x = x_ref[...] + gate_ref[...] * (_rms(y_ref[...]) * gpost_ref[...])
    xo_ref[...] = x
    h = _rms(x) * gpre_ref[...]
    h_ref[...] = (h * (1.0 + sc_ref[...]) + sh_ref[...]).astype(BF16)


def _resid_last_kernel(x_ref, y_ref, gpost_ref, gate_ref, xo_ref):
    xo_ref[...] = x_ref[...] + gate_ref[...] * (_rms(y_ref[...]) * gpost_ref[...])


def _row_spec(tr, D, i0=0):
    return pl.BlockSpec((tr, D), lambda i: (i0 + i, 0))


def _gain_spec(l, D):
    return pl.BlockSpec((None, 1, D), lambda i: (l, 0, 0))


def _mod_spec(l, j, tr, D, i0=0):
    return pl.BlockSpec((None, None, None, 1, D), lambda i: (l, j, ((i0 + i) * tr) // ROW_GROUP, 0, 0))


def _prenorm(x, gains, mods, l, j_sc, j_sh, tr=256):
    M, D = x.shape
    return pl.pallas_call(
        _prenorm_kernel,
        grid=(M // tr,),
        in_specs=[_row_spec(tr, D), _gain_spec(l, D), _mod_spec(l, j_sc, tr, D), _mod_spec(l, j_sh, tr, D)],
        out_specs=_row_spec(tr, D),
        out_shape=jax.ShapeDtypeStruct((M, D), BF16),
        compiler_params=_params("arbitrary"),
        name="prenorm",
    )(x, gains, mods, mods)


def _resid_last(x, y, gpost, mods, l, j_gate, row0, nrows, tr=256):
    D = x.shape[1]
    i0 = row0 // tr
    return pl.pallas_call(
        _resid_last_kernel,
        grid=(nrows // tr,),
        in_specs=[_row_spec(tr, D, i0), _row_spec(tr, D, i0), _gain_spec(l, D),
                  _mod_spec(l, j_gate, tr, D, i0)],
        out_specs=_row_spec(tr, D),
        out_shape=jax.ShapeDtypeStruct((nrows, D), F32),
        compiler_params=_params("arbitrary"),
        name="resid_last",
    )(x, y, gpost, mods)


def _resid(x, y, gpost, mods, l, j_gate, nxt, tr=256):
    M, D = x.shape
    ins = [_row_spec(tr, D), _row_spec(tr, D), _gain_spec(l, D), _mod_spec(l, j_gate, tr, D)]
    args = [x, y, gpost, mods]
    gpre, ln, j_sc, j_sh = nxt
    ins += [_gain_spec(ln, D), _mod_spec(ln, j_sc, tr, D), _mod_spec(ln, j_sh, tr, D)]
    args += [gpre, mods, mods]
    return pl.pallas_call(
        _resid_kernel,
        grid=(M // tr,),
        in_specs=ins,
        out_specs=[_row_spec(tr, D), _row_spec(tr, D)],
        out_shape=[jax.ShapeDtypeStruct((M, D), F32), jax.ShapeDtypeStruct((M, D), BF16)],
        compiler_params=_params("arbitrary"),
        name="resid",
    )(*args)


def _mm_kernel(x_ref, w_ref, o_ref):
    w = w_ref[...].astype(BF16)
    for r0 in range(0, x_ref.shape[0], MM_ROWS):
        rows = slice(r0, min(r0 + MM_ROWS, x_ref.shape[0]))
        o_ref[rows, :] = _bdot(x_ref[rows, :], w).astype(o_ref.dtype)


def _act_spec(tm, K, single_buffer, m0=0):
    if single_buffer:
        return pl.BlockSpec((tm, K), lambda m, n: (m0 + m, 0), pipeline_mode=pl.Buffered(1))
    return pl.BlockSpec((tm, K), lambda m, n: (m0 + m, 0))


def _matmul(x, w, l, out_dtype, tm, tn, name, col0=0, ncols=None, row0=0, nrows=None, single_buffer=False):
    K = x.shape[1]
    M = x.shape[0] - row0 if nrows is None else nrows
    N = w.shape[2] - col0 if ncols is None else ncols
    c0 = col0 // tn
    return pl.pallas_call(
        _mm_kernel,
        grid=(M // tm, N // tn),
        in_specs=[
            _act_spec(tm, K, single_buffer, row0 // tm),
            pl.BlockSpec((None, K, tn), lambda m, n: (l, 0, c0 + n)),
        ],
        out_specs=pl.BlockSpec((tm, tn), lambda m, n: (m, n)),
        out_shape=jax.ShapeDtypeStruct((M, N), out_dtype),
        compiler_params=_params("arbitrary", "arbitrary"),
        name=name,
    )(x, w)


def _proj_in_kernel(x_ref, w_ref, *rest, nq, nf, n_ctx_tiles):
    q_ref, k_ref, v_ref, kv_ref, f_ref, g_ref = rest[-6:]
    m, n = pl.program_id(0), pl.program_id(1)
    is_ctx = m < n_ctx_tiles

    def project(o_ref):
        w = w_ref[...].astype(BF16)
        for r0 in range(0, x_ref.shape[0], MM_ROWS):
            y = _bdot(x_ref[r0:r0 + MM_ROWS, :], w)
            if o_ref.ndim == 3:
                seq = o_ref.shape[1]
                o_ref[r0 // seq:(r0 + MM_ROWS) // seq] = y.reshape(MM_ROWS // seq, seq, y.shape[1])
            else:
                o_ref[r0:r0 + MM_ROWS, :] = y.astype(o_ref.dtype)

    @pl.when(n < nq)
    def _():
        project(q_ref)

    @pl.when((n >= nq) & (n < 2 * nq) & is_ctx)
    def _():
        project(k_ref)

    @pl.when((n >= 2 * nq) & (n < 3 * nq) & is_ctx)
    def _():
        project(v_ref)

    @pl.when((n >= nq) & (n < 3 * nq) & jnp.logical_not(is_ctx))
    def _():
        project(kv_ref)

    @pl.when((n >= 3 * nq) & (n < 3 * nq + nf))
    def _():
        project(f_ref)

    @pl.when(n >= 3 * nq + nf)
    def _():
        project(g_ref)


def _proj_in(h, w, l, caches, n_seq, seq, depth, n_f, tm, tn):
    M, K = h.shape
    N = w.shape[2]
    W = N_HEADS * HEAD_DIM
    nq, nf = W // tn, n_f // tn
    ng = N // tn - 3 * nq - nf
    mc = n_seq * seq // tm
    spt = tm // seq

    def cache_map(first):
        def index(m, n):
            ctx = m < mc
            return (jnp.where(ctx, m, mc - 1), l, 0, jnp.where(ctx, jnp.clip(n - first, 0, nq - 1), nq - 1))
        return index

    def kv_lat_map(m, n):
        return (jnp.maximum(m - mc, 0), jnp.where(m >= mc, jnp.clip(n - nq, 0, 2 * nq - 1), 0))

    in_specs = [
        pl.BlockSpec((tm, K), lambda m, n: (m, 0), pipeline_mode=pl.Buffered(1)),
        pl.BlockSpec((None, K, tn), lambda m, n: (l, 0, n)),
    ]
    args = [h, w]
    aliases = {}
    if caches is not None:
        in_specs += [pl.BlockSpec(memory_space=pl.ANY)] * 2
        args += list(caches)
        aliases = {2: 1, 3: 2}
    cache_shape = jax.ShapeDtypeStruct((n_seq, depth, seq, W), F32)
    kern = functools.partial(_proj_in_kernel, nq=nq, nf=nf, n_ctx_tiles=mc)
    q, new_k, new_v, kv_lat, fxbc, zg = pl.pallas_call(
        kern,
        grid=(M // tm, N // tn),
        in_specs=in_specs,
        out_specs=[
            pl.BlockSpec((tm, tn), lambda m, n: (m, jnp.clip(n, 0, nq - 1))),
            pl.BlockSpec((spt, None, seq, tn), cache_map(nq)),
            pl.BlockSpec((spt, None, seq, tn), cache_map(2 * nq)),
            pl.BlockSpec((tm, tn), kv_lat_map),
            pl.BlockSpec((tm, tn), lambda m, n: (m, jnp.clip(n - 3 * nq, 0, nf - 1))),
            pl.BlockSpec((tm, tn), lambda m, n: (m, jnp.clip(n - 3 * nq - nf, 0, ng - 1))),
        ],
        out_shape=[
            jax.ShapeDtypeStruct((M, W), BF16), cache_shape, cache_shape,
            jax.ShapeDtypeStruct((M - mc * tm, 2 * W), BF16),
            jax.ShapeDtypeStruct((M, n_f), F32),
            jax.ShapeDtypeStruct((M, ng * tn), BF16),
        ],
        input_output_aliases=aliases,
        compiler_params=_params("arbitrary", "arbitrary"),
        name="proj_in",
    )(*args)
    return q, (new_k, new_v), kv_lat, fxbc, zg


def _seq_pos(tm, n_ctx_tiles, seq_ctx, seq_lat):
    seq = jnp.where(pl.program_id(0) < n_ctx_tiles, seq_ctx, seq_lat)
    row = lax.broadcasted_iota(jnp.int32, (tm, 1), 0)
    return row & (seq - 1), seq


def _dwconv3(u, pos, seq, w_ref, b_ref):
    tm = u.shape[0]
    prev = jnp.where(pos == 0, 0.0, pltpu.roll(u, 1, 0))
    nxt = jnp.where(pos == seq - 1, 0.0, pltpu.roll(u, tm - 1, 0))
    return prev * w_ref[0:1, :] + u * w_ref[1:2, :] + nxt * w_ref[2:3, :] + b_ref[...]


def _ffn_up_kernel(h_ref, wg_ref, wv_ref, cw_ref, cb_ref, o_ref, *, n_ctx_tiles, seq_ctx, seq_lat):
    tm = h_ref.shape[0]
    wg = wg_ref[...].astype(BF16)
    wv = wv_ref[...].astype(BF16)
    seq = jnp.where(pl.program_id(0) < n_ctx_tiles, seq_ctx, seq_lat)
    bounds = [0] + [int(tm * f) for f in FFN_SPLITS] + [tm]
    gate = jnp.concatenate([_bdot(h_ref[r0:r1, :], wg) for r0, r1 in zip(bounds[:-1], bounds[1:])], axis=0)
    for r0, r1 in zip(bounds[:-1], bounds[1:]):
        val = _bdot(h_ref[r0:r1, :], wv)
        lo, hi = max(r0 - 8, 0), min(r1 + 8, tm)
        pos = (lo + lax.broadcasted_iota(jnp.int32, (hi - lo, 1), 0)) & (seq - 1)
        g = _dwconv3(gate[lo:hi], pos, seq, cw_ref, cb_ref)[r0 - lo:r1 - lo]
        o_ref[r0:r1, :] = (g * jax.nn.sigmoid(g) * val).astype(o_ref.dtype)


def _ffn_up(h, w_up, conv_w, conv_b, l, n_ctx_rows, seq_ctx, seq_lat, tm=1024, tn=256):
    M, K = h.shape
    L, _, N2 = w_up.shape
    F = N2 // 2
    nb = F // tn
    kern = functools.partial(_ffn_up_kernel, n_ctx_tiles=n_ctx_rows // tm, seq_ctx=seq_ctx, seq_lat=seq_lat)
    return pl.pallas_call(
        kern,
        grid=(M // tm, nb),
        in_specs=[
            pl.BlockSpec((tm, K), lambda m, n: (m, 0)),
            pl.BlockSpec((None, K, tn), lambda m, n: (l, 0, n)),
            pl.BlockSpec((None, K, tn), lambda m, n: (l, 0, nb + n)),
            pl.BlockSpec((None, 3, tn), lambda m, n: (l, 0, n)),
            pl.BlockSpec((None, 1, tn), lambda m, n: (l, 0, n)),
        ],
        out_specs=pl.BlockSpec((tm, tn), lambda m, n: (m, n)),
        out_shape=jax.ShapeDtypeStruct((M, F), BF16),
        compiler_params=_params("arbitrary", "arbitrary"),
        name="ffn_up",
    )(h, w_up, w_up, conv_w, conv_b.reshape(L, 1, F))


def _sconv_kernel(zx_ref, zb_ref, zc_ref, cw_ref, cb_ref, o_ref, *, n_ctx_tiles, seq_ctx, seq_lat):
    u = zc_ref[...] * zx_ref[...]
    pos, seq = _seq_pos(u.shape[0], n_ctx_tiles, seq_ctx, seq_lat)
    o_ref[...] = (zb_ref[...] * _dwconv3(u, pos, seq, cw_ref, cb_ref)).astype(o_ref.dtype)


def _sconv(z, col0, width, conv_w, conv_b, l, n_ctx_rows, seq_ctx, seq_lat, tm=1024, tn=512):
    M = z.shape[0]
    L = conv_w.shape[0]
    kern = functools.partial(_sconv_kernel, n_ctx_tiles=n_ctx_rows // tm, seq_ctx=seq_ctx, seq_lat=seq_lat)
    cb0 = col0 // tn
    wb = width // tn
    return pl.pallas_call(
        kern,
        grid=(M // tm, wb),
        in_specs=[
            pl.BlockSpec((tm, tn), lambda m, n: (m, cb0 + n)),
            pl.BlockSpec((tm, tn), lambda m, n: (m, cb0 + wb + n)),
            pl.BlockSpec((tm, tn), lambda m, n: (m, cb0 + 2 * wb + n)),
            pl.BlockSpec((None, 3, tn), lambda m, n: (l, 0, n)),
            pl.BlockSpec((None, 1, tn), lambda m, n: (l, 0, n)),
        ],
        out_specs=pl.BlockSpec((tm, tn), lambda m, n: (m, n)),
        out_shape=jax.ShapeDtypeStruct((M, width), BF16),
        compiler_params=_params("arbitrary", "arbitrary"),
        name="sconv",
    )(z, z, z, conv_w, conv_b.reshape(L, 1, width))


def _merge_kernel(a_ref, f_ref, c_ref, wa_ref, wf_ref, wc_ref, ga_ref, gf_ref, gc_ref, o_ref):
    wa = wa_ref[...].astype(BF16)
    wf = wf_ref[...].astype(BF16)
    wc = wc_ref[...].astype(BF16)
    for r0 in range(0, o_ref.shape[0], MERGE_CHUNK):
        rows = slice(r0, r0 + MERGE_CHUNK)

        def gated(x_ref, w, g_ref):
            return jax.nn.sigmoid(g_ref[rows, :].astype(F32)) * _bdot(x_ref[rows, :], w)

        o = gated(a_ref, wa, ga_ref) + gated(f_ref, wf, gf_ref) + gated(c_ref, wc, gc_ref)
        o_ref[rows, :] = o.astype(o_ref.dtype)


def _merge(att, fr, cv, w_na_out, w_fnet_out, w_conv_out, z, l, tm=PROJ_TM, tn=PROJ_TN):
    M = att.shape[0]
    D = w_na_out.shape[2]
    gb = D // tn

    def act(a):
        return _act_spec(tm, a.shape[1], True)

    def wgt(w):
        return pl.BlockSpec((None, w.shape[1], tn), lambda m, n: (l, 0, n))

    def gate(j):
        return pl.BlockSpec((tm, tn), lambda m, n: (m, j * gb + n))

    return pl.pallas_call(
        _merge_kernel,
        grid=(M // tm, D // tn),
        in_specs=[act(att), act(fr), act(cv), wgt(w_na_out), wgt(w_fnet_out), wgt(w_conv_out),
                  gate(0), gate(1), gate(2)],
        out_specs=pl.BlockSpec((tm, tn), lambda m, n: (m, n)),
        out_shape=jax.ShapeDtypeStruct((M, D), BF16),
        compiler_params=_params("arbitrary", "arbitrary"),
        name="merge",
    )(att, fr, cv, w_na_out, w_fnet_out, w_conv_out, z, z, z)


def _softmax_pv(s_list, v_list):
    m = functools.reduce(jnp.maximum, [jnp.max(s, axis=-1, keepdims=True) for s in s_list])
    e_list = [jnp.exp(s - m) for s in s_list]
    denom = functools.reduce(jnp.add, [jnp.sum(e, axis=-1, keepdims=True) for e in e_list])
    outs = [_bdot(e.astype(BF16), v) for e, v in zip(e_list, v_list)]
    return functools.reduce(jnp.add, outs) * (1.0 / denom)


def _qk(q, k):
    return lax.dot_general(q, k, (((1,), (1,)), ((), ())), preferred_element_type=F32)


def _ctx_attn_kernel(q_ref, k_ref, v_ref, o_ref, *, scale):
    for h in range(N_HEADS):
        sl = slice(h * HEAD_DIM, (h + 1) * HEAD_DIM)
        q = q_ref[:, sl].astype(BF16)
        k = k_ref[:, sl].astype(BF16)
        v = v_ref[:, sl].astype(BF16)
        s = _qk(q, k) * scale
        o_ref[:, sl] = _softmax_pv([s], [v]).astype(o_ref.dtype)


def _ctx_attention(q, new_k, new_v, l, out_rows):
    n_seq, _, seq, W = new_k.shape
    kern = functools.partial(_ctx_attn_kernel, scale=HEAD_DIM ** -0.5)
    kv_spec = pl.BlockSpec((None, None, seq, W), lambda b: (b, l, 0, 0))
    return pl.pallas_call(
        kern,
        grid=(n_seq,),
        in_specs=[pl.BlockSpec((seq, W), lambda b: (b, 0)), kv_spec, kv_spec],
        out_specs=pl.BlockSpec((seq, W), lambda b: (b, 0)),
        out_shape=jax.ShapeDtypeStruct((out_rows, W), BF16),
        compiler_params=_params("arbitrary"),
        name="ctx_attention",
    )(q, new_k, new_v)


def _window_starts(rows):
    kr = min(NA_ROWS, rows)
    return [min(max(r - kr // 2, 0), rows - kr) for r in range(rows)], kr


def _row_groups(rows):
    starts, kr = _window_starts(rows)
    groups = []
    for r0 in range(0, rows, NA_GROUP_ROWS):
        r1 = min(r0 + NA_GROUP_ROWS, rows)
        groups.append((r0, r1, min(starts[r0:r1]), max(starts[r0:r1]) + kr))
    return groups


def _na_attn_kernel(att_ref, q_ref, k_ref, v_ref, kc_ref, vc_ref, *rest, scale, rows):
    bias_refs, o_ref = rest[:-1], rest[-1]
    q = q_ref[...].astype(BF16)
    k = k_ref[...].astype(BF16)
    v = v_ref[...].astype(BF16)
    vc = vc_ref[...].astype(BF16)
    s_ctx = _qk(q, kc_ref[...].astype(BF16)) * scale
    for (r0, r1, k0, k1), bias_ref in zip(_row_groups(rows), bias_refs):
        qs = slice(r0 * GRID_W, r1 * GRID_W)
        ks = slice(k0 * GRID_W, k1 * GRID_W)
        s_lat = _qk(q[qs], k[ks]) * scale + bias_ref[...]
        o_ref[qs, :] = _softmax_pv([s_lat, s_ctx[qs]], [v[ks], vc]).astype(o_ref.dtype)


def _na_attention(att, q, kv, row0, n_seq, seq, cache_k, cache_v, biases, l):
    P = cache_k.shape[2]
    r0 = row0 // seq
    rows = seq // GRID_W
    kern = functools.partial(_na_attn_kernel, scale=HEAD_DIM ** -0.5, rows=rows)

    def kvspec(j):
        return pl.BlockSpec((seq, HEAD_DIM), lambda h, b: (b, j * N_HEADS + h))

    cspec = pl.BlockSpec((None, None, P, HEAD_DIM), lambda h, b: (b, l, 0, h))
    bspecs = [pl.BlockSpec((None, None) + bias.shape[2:], lambda h, b: (l, h, 0, 0)) for bias in biases]
    return pl.pallas_call(
        kern,
        grid=(N_HEADS, n_seq),
        in_specs=[pl.BlockSpec(memory_space=pl.ANY),
                  pl.BlockSpec((seq, HEAD_DIM), lambda h, b: (r0 + b, h)),
                  kvspec(0), kvspec(1), cspec, cspec] + bspecs,
        out_specs=pl.BlockSpec((seq, HEAD_DIM), lambda h, b: (r0 + b, h)),
        out_shape=jax.ShapeDtypeStruct(att.shape, att.dtype),
        input_output_aliases={0: 0},
        compiler_params=_params("arbitrary", "arbitrary"),
        name="na_attention",
    )(att, q, kv, kv, cache_k, cache_v, *biases)


def _na_bias(rpb, rows):
    starts, kr = _window_starts(rows)
    col = np.arange(GRID_W)
    cs = np.clip(col - NA_COLS // 2, 0, GRID_W - NA_COLS)
    col_ok = (col[None, :] >= cs[:, None]) & (col[None, :] < cs[:, None] + NA_COLS)
    col_idx = np.clip(col[None, :] - col[:, None] + NA_COLS - 1, 0, 2 * NA_COLS - 2)
    onehot = (col_idx[None] == np.arange(2 * NA_COLS - 1)[:, None, None]) & col_ok[None]
    e = jnp.einsum('lhrd,dqk->lhqrk', rpb.astype(F32), jnp.asarray(onehot, F32),
                   precision=lax.Precision.HIGHEST)
    e = jnp.where(jnp.asarray(col_ok)[:, None, :], e, NEG_INF)
    biases = []
    for r0, r1, k0, k1 in _row_groups(rows):
        per_row = []
        for r in range(r0, r1):
            d0 = starts[r] - r + NA_ROWS - 1
            band = e[:, :, :, d0:d0 + kr]
            band = band.reshape(band.shape[:3] + (kr * GRID_W,))
            pad = ((starts[r] - k0) * GRID_W, (k1 - starts[r] - kr) * GRID_W)
            per_row.append(jnp.pad(band, ((0, 0), (0, 0), (0, 0), pad), constant_values=NEG_INF))
        biases.append(jnp.concatenate(per_row, axis=2))
    return biases


def _split_bf16(x):
    hi = x.astype(BF16)
    return hi, (x - hi.astype(F32)).astype(BF16)


def _dot_split(a, b):
    (ah, al), (bh, bl) = a, b
    return _bdot(ah, bh) + _bdot(ah, bl) + _bdot(al, bh)


def _fnet_kernel(x_ref, wch_ref, wcl_ref, wth_ref, wtl_ref, *rest):
    o_ref = rest[-1]
    t = _dot_split(_split_bf16(x_ref[...]), (wch_ref[...], wcl_ref[...]))
    g = t.shape[1] // 2
    u = jnp.concatenate([t[:, :g], t[:, g:]], axis=0)
    y = _dot_split((wth_ref[...], wtl_ref[...]), _split_bf16(u))
    o_ref[...] = y.astype(o_ref.dtype)


def _dft_mats(n):
    k = np.arange(n, dtype=np.int64)
    ang = 2.0 * np.pi * ((k[:, None] * k[None, :]) % n) / n
    s = 1.0 / np.sqrt(n)
    return np.cos(ang) * s, np.sin(ang) * s


def _split_const(w):
    w = jnp.asarray(w, F32)
    return _split_bf16(w)


def _fnet(z, col0, groups, gdim, row0, n_seq, seq, prev=None):
    ct, st = _dft_mats(seq)
    cc, sc = _dft_mats(gdim)
    wch, wcl = _split_const(np.concatenate([cc, sc], axis=1))
    wth, wtl = _split_const(np.concatenate([ct, -st], axis=1))
    r0 = row0 // seq
    c0 = col0 // gdim

    def const(w):
        return pl.BlockSpec(w.shape, lambda b, g: (0, 0))

    in_specs = [pl.BlockSpec((seq, gdim), lambda b, g: (r0 + b, c0 + g)),
                const(wch), const(wcl), const(wth), const(wtl)]
    args = [z, wch, wcl, wth, wtl]
    aliases = {}
    if prev is not None:
        in_specs.append(pl.BlockSpec(memory_space=pl.ANY))
        args.append(prev)
        aliases = {5: 0}
    return pl.pallas_call(
        _fnet_kernel,
        grid=(n_seq, groups),
        in_specs=in_specs,
        out_specs=pl.BlockSpec((seq, gdim), lambda b, g: (r0 + b, g)),
        out_shape=jax.ShapeDtypeStruct((z.shape[0], groups * gdim), BF16),
        input_output_aliases=aliases,
        compiler_params=_params("arbitrary", "arbitrary"),
        name="fnet",
    )(*args)


def kernel(x_prompt, x_sample, cache_k, cache_v, c, c_ctx, w_mod, b_mod, g_pre1, g_post1, g_pre2, g_post2,
           w_in, rpb, w_na_out, w_fnet_out, conv_w, conv_b, w_conv_out, w_o, w_up, ffn_conv_w, ffn_conv_b,
           w_down):
    B, S, D = x_prompt.shape
    Bd, T, _ = x_sample.shape
    L = w_mod.shape[0]
    P = cache_k.shape[2]
    W = N_HEADS * HEAD_DIM
    n_ctx = B * S
    n_lat = Bd * T
    fw = w_fnet_out.shape[1]
    cwid = w_conv_out.shape[1]
    fgroups = 4
    assert T == ROW_GROUP and n_ctx % ROW_GROUP == 0 and Bd + 1 <= 8

    x = jnp.concatenate([x_prompt.reshape(n_ctx, D), x_sample.reshape(n_lat, D)], axis=0)

    c_rows = jnp.zeros((8, D), F32).at[:Bd].set(c).at[Bd].set(c_ctx)
    mod = _modulation(c_rows, w_mod, b_mod)
    group_row = np.concatenate([np.full(n_ctx // ROW_GROUP, Bd), np.arange(Bd)])
    mods = mod.reshape(L, 8, 6, D)[:, group_row]
    mods = mods.transpose(0, 2, 1, 3)[:, :, :, None, :]

    gains = [g.reshape(L, 1, D) for g in (g_pre1, g_post1, g_pre2, g_post2)]
    ck = cache_k.reshape(Bd, L, P, W)
    cv_cache = cache_v.reshape(Bd, L, P, W)

    col_f = 3 * W
    col_gate = col_f + fw + 3 * cwid
    biases = _na_bias(rpb, T // GRID_W)

    new_kv = None
    h = _prenorm(x, gains[0], mods, 0, 1, 0)
    for l in range(L):
        tm, tn = PROJ_TM, PROJ_TN
        q, new_kv, kv_lat, fxbc, zg = _proj_in(h, w_in, l, new_kv, B, S, L, col_gate - col_f, tm, tn)
        new_k, new_v = new_kv
        att = _ctx_attention(q, new_k, new_v, l, n_ctx + n_lat)
        att = _na_attention(att, q, kv_lat, n_ctx, Bd, T, ck, cv_cache, biases, l)
        fr = _fnet(fxbc, 0, fgroups, fw // fgroups, 0, B, S)
        fr = _fnet(fxbc, 0, fgroups, fw // fgroups, n_ctx, Bd, T, prev=fr)
        cv = _sconv(fxbc, fw, cwid, conv_w, conv_b, l, n_ctx, S, T)
        merged = _merge(att, fr, cv, w_na_out, w_fnet_out, w_conv_out, zg, l)
        y = _matmul(merged, w_o, l, F32, tm, tn, "proj_o")
        x, h = _resid(x, y, gains[1], mods, l, 2, (gains[2], l, 4, 3))
        act = _ffn_up(h, w_up, ffn_conv_w, ffn_conv_b, l, n_ctx, S, T)
        y = _matmul(act, w_down, l, F32, 1024, 256, "ffn_down", single_buffer=True)
        if l + 1 < L:
            x, h = _resid(x, y, gains[3], mods, l, 5, (gains[0], l + 1, 1, 0))

    y_prompt = _resid_last(x, y, gains[3], mods, L - 1, 5, 0, n_ctx).reshape(B, S, D)
    y_sample = _resid_last(x, y, gains[3], mods, L - 1, 5, n_ctx, n_lat).reshape(Bd, T, D)
    kv_shape = (B, L, S, N_HEADS, HEAD_DIM)
    return y_prompt, y_sample, new_k.reshape(kv_shape), new_v.reshape(kv_shape)
```

```python
import functools

import numpy as np
import jax
import jax.numpy as jnp
from jax import lax
from jax.experimental import pallas as pl
from jax.experimental.pallas import tpu as pltpu

F32 = jnp.float32
BF16 = jnp.bfloat16

EPS = 1e-6
NEG_INF = -1e30
GRID_W = 64
NA_ROWS = 8
NA_COLS = 16
N_HEADS = 16
HEAD_DIM = 128
MERGE_CHUNK = 256
FFN_SPLITS = (0.375, 0.75)
NA_GROUP_ROWS = 4

VMEM_LIMIT_BYTES = 58 * 1024 * 1024
ROW_GROUP = 1024
PROJ_TM, PROJ_TN = 2048, 256
MM_ROWS = 1024


def _params(*sem):
    return pltpu.CompilerParams(dimension_semantics=sem, vmem_limit_bytes=VMEM_LIMIT_BYTES)


def _bdot(a, b):
    return jnp.dot(a, b, preferred_element_type=F32)


def _mod_kernel(c_ref, w_ref, b_ref, o_ref):
    c = c_ref[...]
    s = (c * jax.nn.sigmoid(c)).astype(BF16)
    o_ref[...] = _bdot(s, w_ref[...].astype(BF16)) + b_ref[...]


def _modulation(c_rows, w_mod, b_mod, tn=512):
    L, D, N = w_mod.shape
    return pl.pallas_call(
        _mod_kernel,
        grid=(L, N // tn),
        in_specs=[
            pl.BlockSpec((8, D), lambda l, n: (0, 0)),
            pl.BlockSpec((None, D, tn), lambda l, n: (l, 0, n)),
            pl.BlockSpec((None, 1, tn), lambda l, n: (l, 0, n)),
        ],
        out_specs=pl.BlockSpec((None, 8, tn), lambda l, n: (l, 0, n)),
        out_shape=jax.ShapeDtypeStruct((L, 8, N), F32),
        compiler_params=_params("arbitrary", "arbitrary"),
        name="modulation",
    )(c_rows, w_mod, b_mod.reshape(L, 1, N))


def _rms(x):
    return x * lax.rsqrt(jnp.mean(x * x, axis=-1, keepdims=True) + EPS)


def _prenorm_kernel(xa_ref, xb_ref, g_ref, sc_ref, sh_ref, x_ref, h_ref, *, na_tiles):
    x = jnp.where(pl.program_id(0) < na_tiles, xa_ref[...], xb_ref[...])
    x_ref[...] = x
    h = _rms(x) * g_ref[...]
    h_ref[...] = (h * (1.0 + sc_ref[...]) + sh_ref[...]).astype(BF16)


def _resid_kernel(x_ref, y_ref, gpost_ref, gate_ref, gpre_ref, sc_ref, sh_ref, xo_ref, h_ref):
    x = x_ref[...] + gate_ref[...] * (_rms(y_ref[...]) * gpost_ref[...])
    xo_ref[...] = x
    h = _rms(x) * gpre_ref[...]
    h_ref[...] = (h * (1.0 + sc_ref[...]) + sh_ref[...]).astype(BF16)


def _resid_last_kernel(x_ref, y_ref, gpost_ref, gate_ref, xo_ref):
    xo_ref[...] = x_ref[...] + gate_ref[...] * (_rms(y_ref[...]) * gpost_ref[...])


def _row_spec(tr, D, i0=0):
    return pl.BlockSpec((tr, D), lambda i: (i0 + i, 0))


def _gain_spec(l, D):
    return pl.BlockSpec((None, 1, D), lambda i: (l, 0, 0))


def _mod_spec(l, j, tr, D, i0=0):
    return pl.BlockSpec((None, None, None, 1, D), lambda i: (l, j, ((i0 + i) * tr) // ROW_GROUP, 0, 0))


def _prenorm(xa, xb, gains, mods, l, j_sc, j_sh, tr=256):
    D = xa.shape[1]
    na = xa.shape[0] // tr
    M = xa.shape[0] + xb.shape[0]
    kern = functools.partial(_prenorm_kernel, na_tiles=na)
    return pl.pallas_call(
        kern,
        grid=(M // tr,),
        in_specs=[pl.BlockSpec((tr, D), lambda i: (jnp.minimum(i, na - 1), 0)),
                  pl.BlockSpec((tr, D), lambda i: (jnp.maximum(i - na, 0), 0)),
                  _gain_spec(l, D), _mod_spec(l, j_sc, tr, D), _mod_spec(l, j_sh, tr, D)],
        out_specs=[_row_spec(tr, D), _row_spec(tr, D)],
        out_shape=[jax.ShapeDtypeStruct((M, D), F32), jax.ShapeDtypeStruct((M, D), BF16)],
        compiler_params=_params("arbitrary"),
        name="prenorm",
    )(xa, xb, gains, mods, mods)


def _resid_last(x, y, gpost, mods, l, j_gate, row0, nrows, tr=256):
    D = x.shape[1]
    i0 = row0 // tr
    return pl.pallas_call(
        _resid_last_kernel,
        grid=(nrows // tr,),
        in_specs=[_row_spec(tr, D, i0), _row_spec(tr, D, i0), _gain_spec(l, D),
                  _mod_spec(l, j_gate, tr, D, i0)],
        out_specs=_row_spec(tr, D),
        out_shape=jax.ShapeDtypeStruct((nrows, D), F32),
        compiler_params=_params("arbitrary"),
        name="resid_last",
    )(x, y, gpost, mods)


def _resid(x, y, gpost, mods, l, j_gate, nxt, tr=256):
    M, D = x.shape
    ins = [_row_spec(tr, D), _row_spec(tr, D), _gain_spec(l, D), _mod_spec(l, j_gate, tr, D)]
    args = [x, y, gpost, mods]
    gpre, ln, j_sc, j_sh = nxt
    ins += [_gain_spec(ln, D), _mod_spec(ln, j_sc, tr, D), _mod_spec(ln, j_sh, tr, D)]
    args += [gpre, mods, mods]
    return pl.pallas_call(
        _resid_kernel,
        grid=(M // tr,),
        in_specs=ins,
        out_specs=[_row_spec(tr, D), _row_spec(tr, D)],
        out_shape=[jax.ShapeDtypeStruct((M, D), F32), jax.ShapeDtypeStruct((M, D), BF16)],
        compiler_params=_params("arbitrary"),
        name="resid",
    )(*args)


def _mm_kernel(x_ref, w_ref, o_ref):
    w = w_ref[...].astype(BF16)
    for r0 in range(0, x_ref.shape[0], MM_ROWS):
        rows = slice(r0, min(r0 + MM_ROWS, x_ref.shape[0]))
        o_ref[rows, :] = _bdot(x_ref[rows, :], w).astype(o_ref.dtype)


def _act_spec(tm, K, single_buffer, m0=0):
    if single_buffer:
        return pl.BlockSpec((tm, K), lambda m, n: (m0 + m, 0), pipeline_mode=pl.Buffered(1))
    return pl.BlockSpec((tm, K), lambda m, n: (m0 + m, 0))


def _matmul(x, w, l, out_dtype, tm, tn, name, col0=0, ncols=None, row0=0, nrows=None, single_buffer=False):
    K = x.shape[1]
    M = x.shape[0] - row0 if nrows is None else nrows
    N = w.shape[2] - col0 if ncols is None else ncols
    c0 = col0 // tn
    return pl.pallas_call(
        _mm_kernel,
        grid=(M // tm, N // tn),
        in_specs=[
            _act_spec(tm, K, single_buffer, row0 // tm),
            pl.BlockSpec((None, K, tn), lambda m, n: (l, 0, c0 + n)),
        ],
        out_specs=pl.BlockSpec((tm, tn), lambda m, n: (m, n)),
        out_shape=jax.ShapeDtypeStruct((M, N), out_dtype),
        compiler_params=_params("arbitrary", "arbitrary"),
        name=name,
    )(x, w)


def _proj_in_kernel(x_ref, w_ref, *rest, nq, nf, n_ctx_tiles):
    q_ref, k_ref, v_ref, kv_ref, f_ref, g_ref = rest[-6:]
    m, n = pl.program_id(0), pl.program_id(1)
    is_ctx = m < n_ctx_tiles

    def project(o_ref):
        w = w_ref[...].astype(BF16)
        for r0 in range(0, x_ref.shape[0], MM_ROWS):
            y = _bdot(x_ref[r0:r0 + MM_ROWS, :], w)
            if o_ref.ndim == 3:
                seq = o_ref.shape[1]
                o_ref[r0 // seq:(r0 + MM_ROWS) // seq] = y.reshape(MM_ROWS // seq, seq, y.shape[1])
            else:
                o_ref[r0:r0 + MM_ROWS, :] = y.astype(o_ref.dtype)

    @pl.when(n < nq)
    def _():
        project(q_ref)

    @pl.when((n >= nq) & (n < 2 * nq) & is_ctx)
    def _():
        project(k_ref)

    @pl.when((n >= 2 * nq) & (n < 3 * nq) & is_ctx)
    def _():
        project(v_ref)

    @pl.when((n >= nq) & (n < 3 * nq) & jnp.logical_not(is_ctx))
    def _():
        project(kv_ref)

    @pl.when((n >= 3 * nq) & (n < 3 * nq + nf))
    def _():
        project(f_ref)

    @pl.when(n >= 3 * nq + nf)
    def _():
        project(g_ref)


def _proj_in(h, w, l, caches, n_seq, seq, depth, n_f, tm, tn):
    M, K = h.shape
    N = w.shape[2]
    W = N_HEADS * HEAD_DIM
    nq, nf = W // tn, n_f // tn
    ng = N // tn - 3 * nq - nf
    mc = n_seq * seq // tm
    spt = tm // seq

    def cache_map(first):
        def index(m, n):
            ctx = m < mc
            return (jnp.where(ctx, m, mc - 1), l, 0, jnp.where(ctx, jnp.clip(n - first, 0, nq - 1), nq - 1))
        return index

    def kv_lat_map(m, n):
        return (jnp.maximum(m - mc, 0), jnp.where(m >= mc, jnp.clip(n - nq, 0, 2 * nq - 1), 0))

    in_specs = [
        pl.BlockSpec((tm, K), lambda m, n: (m, 0), pipeline_mode=pl.Buffered(1)),
        pl.BlockSpec((None, K, tn), lambda m, n: (l, 0, n)),
    ]
    args = [h, w]
    aliases = {}
    if caches is not None:
        in_specs += [pl.BlockSpec(memory_space=pl.ANY)] * 2
        args += list(caches)
        aliases = {2: 1, 3: 2}
    cache_shape = jax.ShapeDtypeStruct((n_seq, depth, seq, W), F32)
    kern = functools.partial(_proj_in_kernel, nq=nq, nf=nf, n_ctx_tiles=mc)
    q, new_k, new_v, kv_lat, fxbc, zg = pl.pallas_call(
        kern,
        grid=(M // tm, N // tn),
        in_specs=in_specs,
        out_specs=[
            pl.BlockSpec((tm, tn), lambda m, n: (m, jnp.clip(n, 0, nq - 1))),
            pl.BlockSpec((spt, None, seq, tn), cache_map(nq)),
            pl.BlockSpec((spt, None, seq, tn), cache_map(2 * nq)),
            pl.BlockSpec((tm, tn), kv_lat_map),
            pl.BlockSpec((tm, tn), lambda m, n: (m, jnp.clip(n - 3 * nq, 0, nf - 1))),
            pl.BlockSpec((tm, tn), lambda m, n: (m, jnp.clip(n - 3 * nq - nf, 0, ng - 1))),
        ],
        out_shape=[
            jax.ShapeDtypeStruct((M, W), BF16), cache_shape, cache_shape,
            jax.ShapeDtypeStruct((M - mc * tm, 2 * W), BF16),
            jax.ShapeDtypeStruct((M, n_f), F32),
            jax.ShapeDtypeStruct((M, ng * tn), BF16),
        ],
        input_output_aliases=aliases,
        compiler_params=_params("arbitrary", "arbitrary"),
        name="proj_in",
    )(*args)
    return q, (new_k, new_v), kv_lat, fxbc, zg


def _seq_pos(tm, n_ctx_tiles, seq_ctx, seq_lat):
    seq = jnp.where(pl.program_id(0) < n_ctx_tiles, seq_ctx, seq_lat)
    row = lax.broadcasted_iota(jnp.int32, (tm, 1), 0)
    return row & (seq - 1), seq


def _dwconv3(u, pos, seq, w_ref, b_ref):
    tm = u.shape[0]
    prev = jnp.where(pos == 0, 0.0, pltpu.roll(u, 1, 0))
    nxt = jnp.where(pos == seq - 1, 0.0, pltpu.roll(u, tm - 1, 0))
    return prev * w_ref[0:1, :] + u * w_ref[1:2, :] + nxt * w_ref[2:3, :] + b_ref[...]


def _ffn_up_kernel(h_ref, wg_ref, wv_ref, cw_ref, cb_ref, o_ref, *, n_ctx_tiles, seq_ctx, seq_lat):
    tm = h_ref.shape[0]
    wg = wg_ref[...].astype(BF16)
    wv = wv_ref[...].astype(BF16)
    seq = jnp.where(pl.program_id(0) < n_ctx_tiles, seq_ctx, seq_lat)
    bounds = [0] + [int(tm * f) for f in FFN_SPLITS] + [tm]
    gate = jnp.concatenate([_bdot(h_ref[r0:r1, :], wg) for r0, r1 in zip(bounds[:-1], bounds[1:])], axis=0)
    for r0, r1 in zip(bounds[:-1], bounds[1:]):
        val = _bdot(h_ref[r0:r1, :], wv)
        lo, hi = max(r0 - 8, 0), min(r1 + 8, tm)
        pos = (lo + lax.broadcasted_iota(jnp.int32, (hi - lo, 1), 0)) & (seq - 1)
        g = _dwconv3(gate[lo:hi], pos, seq, cw_ref, cb_ref)[r0 - lo:r1 - lo]
        o_ref[r0:r1, :] = (g * jax.nn.sigmoid(g) * val).astype(o_ref.dtype)


def _ffn_up(h, w_up, conv_w, conv_b, l, n_ctx_rows, seq_ctx, seq_lat, tm=PROJ_TM, tn=PROJ_TN):
    M, K = h.shape
    L, _, N2 = w_up.shape
    F = N2 // 2
    nb = F // tn
    kern = functools.partial(_ffn_up_kernel, n_ctx_tiles=n_ctx_rows // tm, seq_ctx=seq_ctx, seq_lat=seq_lat)
    return pl.pallas_call(
        kern,
        grid=(M // tm, nb),
        in_specs=[
            _act_spec(tm, K, True),
            pl.BlockSpec((None, K, tn), lambda m, n: (l, 0, n)),
            pl.BlockSpec((None, K, tn), lambda m, n: (l, 0, nb + n)),
            pl.BlockSpec((None, 3, tn), lambda m, n: (l, 0, n)),
            pl.BlockSpec((None, 1, tn), lambda m, n: (l, 0, n)),
        ],
        out_specs=pl.BlockSpec((tm, tn), lambda m, n: (m, n)),
        out_shape=jax.ShapeDtypeStruct((M, F), BF16),
        compiler_params=_params("arbitrary", "arbitrary"),
        name="ffn_up",
    )(h, w_up, w_up, conv_w, conv_b.reshape(L, 1, F))


def _sconv_kernel(zx_ref, zb_ref, zc_ref, cw_ref, cb_ref, o_ref, *, n_ctx_tiles, seq_ctx, seq_lat):
    u = zc_ref[...] * zx_ref[...]
    pos, seq = _seq_pos(u.shape[0], n_ctx_tiles, seq_ctx, seq_lat)
    o_ref[...] = (zb_ref[...] * _dwconv3(u, pos, seq, cw_ref, cb_ref)).astype(o_ref.dtype)


def _sconv(z, col0, width, conv_w, conv_b, l, n_ctx_rows, seq_ctx, seq_lat, tm=1024, tn=512):
    M = z.shape[0]
    L = conv_w.shape[0]
    kern = functools.partial(_sconv_kernel, n_ctx_tiles=n_ctx_rows // tm, seq_ctx=seq_ctx, seq_lat=seq_lat)
    cb0 = col0 // tn
    wb = width // tn
    return pl.pallas_call(
        kern,
        grid=(M // tm, wb),
        in_specs=[
            pl.BlockSpec((tm, tn), lambda m, n: (m, cb0 + n)),
            pl.BlockSpec((tm, tn), lambda m, n: (m, cb0 + wb + n)),
            pl.BlockSpec((tm, tn), lambda m, n: (m, cb0 + 2 * wb + n)),
            pl.BlockSpec((None, 3, tn), lambda m, n: (l, 0, n)),
            pl.BlockSpec((None, 1, tn), lambda m, n: (l, 0, n)),
        ],
        out_specs=pl.BlockSpec((tm, tn), lambda m, n: (m, n)),
        out_shape=jax.ShapeDtypeStruct((M, width), BF16),
        compiler_params=_params("arbitrary", "arbitrary"),
        name="sconv",
    )(z, z, z, conv_w, conv_b.reshape(L, 1, width))


def _merge_kernel(a_ref, f_ref, c_ref, wa_ref, wf_ref, wc_ref, ga_ref, gf_ref, gc_ref, o_ref):
    wa = wa_ref[...].astype(BF16)
    wf = wf_ref[...].astype(BF16)
    wc = wc_ref[...].astype(BF16)
    for r0 in range(0, o_ref.shape[0], MERGE_CHUNK):
        rows = slice(r0, r0 + MERGE_CHUNK)

        def gated(x_ref, w, g_ref):
            return jax.nn.sigmoid(g_ref[rows, :].astype(F32)) * _bdot(x_ref[rows, :], w)

        o = gated(a_ref, wa, ga_ref) + gated(f_ref, wf, gf_ref) + gated(c_ref, wc, gc_ref)
        o_ref[rows, :] = o.astype(o_ref.dtype)


def _merge(att, fr, cv, w_na_out, w_fnet_out, w_conv_out, z, l, tm=PROJ_TM, tn=PROJ_TN):
    M = att.shape[0]
    D = w_na_out.shape[2]
    gb = D // tn

    def act(a):
        return _act_spec(tm, a.shape[1], True)

    def wgt(w):
        return pl.BlockSpec((None, w.shape[1], tn), lambda m, n: (l, 0, n))

    def gate(j):
        return pl.BlockSpec((tm, tn), lambda m, n: (m, j * gb + n))

    return pl.pallas_call(
        _merge_kernel,
        grid=(M // tm, D // tn),
        in_specs=[act(att), act(fr), act(cv), wgt(w_na_out), wgt(w_fnet_out), wgt(w_conv_out),
                  gate(0), gate(1), gate(2)],
        out_specs=pl.BlockSpec((tm, tn), lambda m, n: (m, n)),
        out_shape=jax.ShapeDtypeStruct((M, D), BF16),
        compiler_params=_params("arbitrary", "arbitrary"),
        name="merge",
    )(att, fr, cv, w_na_out, w_fnet_out, w_conv_out, z, z, z)


def _softmax_pv(s_list, v_list):
    m = functools.reduce(jnp.maximum, [jnp.max(s, axis=-1, keepdims=True) for s in s_list])
    e_list = [jnp.exp(s - m) for s in s_list]
    denom = functools.reduce(jnp.add, [jnp.sum(e, axis=-1, keepdims=True) for e in e_list])
    outs = [_bdot(e.astype(BF16), v) for e, v in zip(e_list, v_list)]
    return functools.reduce(jnp.add, outs) * (1.0 / denom)


def _qk(q, k):
    return lax.dot_general(q, k, (((1,), (1,)), ((), ())), preferred_element_type=F32)


def _ctx_attn_kernel(q_ref, k_ref, v_ref, o_ref, *, scale):
    for h in range(N_HEADS):
        sl = slice(h * HEAD_DIM, (h + 1) * HEAD_DIM)
        q = q_ref[:, sl].astype(BF16)
        k = k_ref[:, sl].astype(BF16)
        v = v_ref[:, sl].astype(BF16)
        s = _qk(q, k) * scale
        o_ref[:, sl] = _softmax_pv([s], [v]).astype(o_ref.dtype)


def _ctx_attention(q, new_k, new_v, l, out_rows):
    n_seq, _, seq, W = new_k.shape
    kern = functools.partial(_ctx_attn_kernel, scale=HEAD_DIM ** -0.5)
    kv_spec = pl.BlockSpec((None, None, seq, W), lambda b: (b, l, 0, 0))
    return pl.pallas_call(
        kern,
        grid=(n_seq,),
        in_specs=[pl.BlockSpec((seq, W), lambda b: (b, 0)), kv_spec, kv_spec],
        out_specs=pl.BlockSpec((seq, W), lambda b: (b, 0)),
        out_shape=jax.ShapeDtypeStruct((out_rows, W), BF16),
        compiler_params=_params("arbitrary"),
        name="ctx_attention",
    )(q, new_k, new_v)


def _window_starts(rows):
    kr = min(NA_ROWS, rows)
    return [min(max(r - kr // 2, 0), rows - kr) for r in range(rows)], kr


def _row_groups(rows):
    starts, kr = _window_starts(rows)
    groups = []
    for r0 in range(0, rows, NA_GROUP_ROWS):
        r1 = min(r0 + NA_GROUP_ROWS, rows)
        groups.append((r0, r1, min(starts[r0:r1]), max(starts[r0:r1]) + kr))
    return groups


def _na_attn_kernel(att_ref, q_ref, k_ref, v_ref, kc_ref, vc_ref, *rest, scale, rows):
    bias_refs, o_ref = rest[:-1], rest[-1]
    q = q_ref[...].astype(BF16)
    k = k_ref[...].astype(BF16)
    v = v_ref[...].astype(BF16)
    vc = vc_ref[...].astype(BF16)
    s_ctx = _qk(q, kc_ref[...].astype(BF16)) * scale
    for (r0, r1, k0, k1), bias_ref in zip(_row_groups(rows), bias_refs):
        qs = slice(r0 * GRID_W, r1 * GRID_W)
        ks = slice(k0 * GRID_W, k1 * GRID_W)
        s_lat = _qk(q[qs], k[ks]) * scale + bias_ref[...]
        o_ref[qs, :] = _softmax_pv([s_lat, s_ctx[qs]], [v[ks], vc]).astype(o_ref.dtype)


def _na_attention(att, q, kv, row0, n_seq, seq, cache_k, cache_v, biases, l):
    P = cache_k.shape[2]
    r0 = row0 // seq
    rows = seq // GRID_W
    kern = functools.partial(_na_attn_kernel, scale=HEAD_DIM ** -0.5, rows=rows)

    def kvspec(j):
        return pl.BlockSpec((seq, HEAD_DIM), lambda h, b: (b, j * N_HEADS + h))

    cspec = pl.BlockSpec((None, None, P, HEAD_DIM), lambda h, b: (b, l, 0, h))
    bspecs = [pl.BlockSpec((None, None) + bias.shape[2:], lambda h, b: (l, h, 0, 0)) for bias in biases]
    return pl.pallas_call(
        kern,
        grid=(N_HEADS, n_seq),
        in_specs=[pl.BlockSpec(memory_space=pl.ANY),
                  pl.BlockSpec((seq, HEAD_DIM), lambda h, b: (r0 + b, h)),
                  kvspec(0), kvspec(1), cspec, cspec] + bspecs,
        out_specs=pl.BlockSpec((seq, HEAD_DIM), lambda h, b: (r0 + b, h)),
        out_shape=jax.ShapeDtypeStruct(att.shape, att.dtype),
        input_output_aliases={0: 0},
        compiler_params=_params("arbitrary", "arbitrary"),
        name="na_attention",
    )(att, q, kv, kv, cache_k, cache_v, *biases)


def _na_bias(rpb, rows):
    starts, kr = _window_starts(rows)
    col = np.arange(GRID_W)
    cs = np.clip(col - NA_COLS // 2, 0, GRID_W - NA_COLS)
    col_ok = (col[None, :] >= cs[:, None]) & (col[None, :] < cs[:, None] + NA_COLS)
    col_idx = np.clip(col[None, :] - col[:, None] + NA_COLS - 1, 0, 2 * NA_COLS - 2)
    onehot = (col_idx[None] == np.arange(2 * NA_COLS - 1)[:, None, None]) & col_ok[None]
    e = jnp.einsum('lhrd,dqk->lhqrk', rpb.astype(F32), jnp.asarray(onehot, F32),
                   precision=lax.Precision.HIGHEST)
    e = jnp.where(jnp.asarray(col_ok)[:, None, :], e, NEG_INF)
    biases = []
    for r0, r1, k0, k1 in _row_groups(rows):
        per_row = []
        for r in range(r0, r1):
            d0 = starts[r] - r + NA_ROWS - 1
            band = e[:, :, :, d0:d0 + kr]
            band = band.reshape(band.shape[:3] + (kr * GRID_W,))
            pad = ((starts[r] - k0) * GRID_W, (k1 - starts[r] - kr) * GRID_W)
            per_row.append(jnp.pad(band, ((0, 0), (0, 0), (0, 0), pad), constant_values=NEG_INF))
        biases.append(jnp.concatenate(per_row, axis=2))
    return biases


def _split_bf16(x):
    hi = x.astype(BF16)
    return hi, (x - hi.astype(F32)).astype(BF16)


def _dot_split(a, b):
    (ah, al), (bh, bl) = a, b
    return _bdot(ah, bh) + _bdot(ah, bl) + _bdot(al, bh)


def _fnet_kernel(x_ref, wch_ref, wcl_ref, wth_ref, wtl_ref, *rest):
    o_ref = rest[-1]
    t = _dot_split(_split_bf16(x_ref[...]), (wch_ref[...], wcl_ref[...]))
    g = t.shape[1] // 2
    u = jnp.concatenate([t[:, :g], t[:, g:]], axis=0)
    y = _dot_split((wth_ref[...], wtl_ref[...]), _split_bf16(u))
    o_ref[...] = y.astype(o_ref.dtype)


def _dft_mats(n):
    k = np.arange(n, dtype=np.int64)
    ang = 2.0 * np.pi * ((k[:, None] * k[None, :]) % n) / n
    s = 1.0 / np.sqrt(n)
    return np.cos(ang) * s, np.sin(ang) * s


def _split_const(w):
    w = jnp.asarray(w, F32)
    return _split_bf16(w)


def _fnet(z, col0, groups, gdim, row0, n_seq, seq, prev=None):
    ct, st = _dft_mats(seq)
    cc, sc = _dft_mats(gdim)
    wch, wcl = _split_const(np.concatenate([cc, sc], axis=1))
    wth, wtl = _split_const(np.concatenate([ct, -st], axis=1))
    r0 = row0 // seq
    c0 = col0 // gdim

    def const(w):
        return pl.BlockSpec(w.shape, lambda b, g: (0, 0))

    in_specs = [pl.BlockSpec((seq, gdim), lambda b, g: (r0 + b, c0 + g)),
                const(wch), const(wcl), const(wth), const(wtl)]
    args = [z, wch, wcl, wth, wtl]
    aliases = {}
    if prev is not None:
        in_specs.append(pl.BlockSpec(memory_space=pl.ANY))
        args.append(prev)
        aliases = {5: 0}
    return pl.pallas_call(
        _fnet_kernel,
        grid=(n_seq, groups),
        in_specs=in_specs,
        out_specs=pl.BlockSpec((seq, gdim), lambda b, g: (r0 + b, g)),
        out_shape=jax.ShapeDtypeStruct((z.shape[0], groups * gdim), BF16),
        input_output_aliases=aliases,
        compiler_params=_params("arbitrary", "arbitrary"),
        name="fnet",
    )(*args)


def kernel(x_prompt, x_sample, cache_k, cache_v, c, c_ctx, w_mod, b_mod, g_pre1, g_post1, g_pre2, g_post2,
           w_in, rpb, w_na_out, w_fnet_out, conv_w, conv_b, w_conv_out, w_o, w_up, ffn_conv_w, ffn_conv_b,
           w_down):
    B, S, D = x_prompt.shape
    Bd, T, _ = x_sample.shape
    L = w_mod.shape[0]
    P = cache_k.shape[2]
    W = N_HEADS * HEAD_DIM
    n_ctx = B * S
    n_lat = Bd * T
    fw = w_fnet_out.shape[1]
    cwid = w_conv_out.shape[1]
    fgroups = 4
    assert T == ROW_GROUP and n_ctx % ROW_GROUP == 0 and Bd + 1 <= 8

    c_rows = jnp.zeros((8, D), F32).at[:Bd].set(c).at[Bd].set(c_ctx)
    mod = _modulation(c_rows, w_mod, b_mod)
    group_row = np.concatenate([np.full(n_ctx // ROW_GROUP, Bd), np.arange(Bd)])
    mods = mod.reshape(L, 8, 6, D)[:, group_row]
    mods = mods.transpose(0, 2, 1, 3)[:, :, :, None, :]

    gains = [g.reshape(L, 1, D) for g in (g_pre1, g_post1, g_pre2, g_post2)]
    ck = cache_k.reshape(Bd, L, P, W)
    cv_cache = cache_v.reshape(Bd, L, P, W)

    col_f = 3 * W
    col_gate = col_f + fw + 3 * cwid
    biases = _na_bias(rpb, T // GRID_W)

    new_kv = None
    x, h = _prenorm(x_prompt.reshape(n_ctx, D), x_sample.reshape(n_lat, D), gains[0], mods, 0, 1, 0)
    for l in range(L):
        tm, tn = PROJ_TM, PROJ_TN
        q, new_kv, kv_lat, fxbc, zg = _proj_in(h, w_in, l, new_kv, B, S, L, col_gate - col_f, tm, tn)
        new_k, new_v = new_kv
        att = _ctx_attention(q, new_k, new_v, l, n_ctx + n_lat)
        att = _na_attention(att, q, kv_lat, n_ctx, Bd, T, ck, cv_cache, biases, l)
        fr = _fnet(fxbc, 0, fgroups, fw // fgroups, 0, B, S)
        fr = _fnet(fxbc, 0, fgroups, fw // fgroups, n_ctx, Bd, T, prev=fr)
        cv = _sconv(fxbc, fw, cwid, conv_w, conv_b, l, n_ctx, S, T)
        merged = _merge(att, fr, cv, w_na_out, w_fnet_out, w_conv_out, zg, l)
        y = _matmul(merged, w_o, l, F32, tm, tn, "proj_o")
        x, h = _resid(x, y, gains[1], mods, l, 2, (gains[2], l, 4, 3))
        act = _ffn_up(h, w_up, ffn_conv_w, ffn_conv_b, l, n_ctx, S, T)
        y = _matmul(act, w_down, l, F32, 1024, 256, "ffn_down", single_buffer=True)
        if l + 1 < L:
            x, h = _resid(x, y, gains[3], mods, l, 5, (gains[0], l + 1, 1, 0))

    y_prompt = _resid_last(x, y, gains[3], mods, L - 1, 5, 0, n_ctx).reshape(B, S, D)
    y_sample = _resid_last(x, y, gains[3], mods, L - 1, 5, n_ctx, n_lat).reshape(Bd, T, D)
    kv_shape = (B, L, S, N_HEADS, HEAD_DIM)
    return y_prompt, y_sample, new_k.reshape(kv_shape), new_v.reshape(kv_shape)
```

```python
import functools

import numpy as np
import jax
import jax.numpy as jnp
from jax import lax
from jax.experimental import pallas as pl
from jax.experimental.pallas import tpu as pltpu

F32 = jnp.float32
BF16 = jnp.bfloat16

EPS = 1e-6
NEG_INF = -1e30
GRID_W = 64
NA_ROWS = 8
NA_COLS = 16
N_HEADS = 16
HEAD_DIM = 128
MERGE_CHUNK = 256
FFN_SPLITS = (0.375, 0.75)
NA_GROUP_ROWS = 4

VMEM_LIMIT_BYTES = 58 * 1024 * 1024
ROW_GROUP = 1024
PROJ_TM, PROJ_TN = 2048, 256
MM_ROWS = 1024


def _params(*sem):
    return pltpu.CompilerParams(dimension_semantics=sem, vmem_limit_bytes=VMEM_LIMIT_BYTES)


def _bdot(a, b):
    return jnp.dot(a, b, preferred_element_type=F32)


def _mod_kernel(c_ref, w_ref, b_ref, o_ref):
    c = c_ref[...]
    s = (c * jax.nn.sigmoid(c)).astype(BF16)
    o_ref[...] = _bdot(s, w_ref[...].astype(BF16)) + b_ref[...]


def _modulation(c_rows, w_mod, b_mod, tn=512):
    L, D, N = w_mod.shape
    return pl.pallas_call(
        _mod_kernel,
        grid=(L, N // tn),
        in_specs=[
            pl.BlockSpec((8, D), lambda l, n: (0, 0)),
            pl.BlockSpec((None, D, tn), lambda l, n: (l, 0, n)),
            pl.BlockSpec((None, 1, tn), lambda l, n: (l, 0, n)),
        ],
        out_specs=pl.BlockSpec((None, 8, tn), lambda l, n: (l, 0, n)),
        out_shape=jax.ShapeDtypeStruct((L, 8, N), F32),
        compiler_params=_params("arbitrary", "arbitrary"),
        name="modulation",
    )(c_rows, w_mod, b_mod.reshape(L, 1, N))


def _rms(x):
    return x * lax.rsqrt(jnp.mean(x * x, axis=-1, keepdims=True) + EPS)


def _prenorm_kernel(xa_ref, xb_ref, g_ref, sc_ref, sh_ref, x_ref, h_ref, *, na_tiles):
    x = jnp.where(pl.program_id(0) < na_tiles, xa_ref[...], xb_ref[...])
    x_ref[...] = x
    h = _rms(x) * g_ref[...]
    h_ref[...] = (h * (1.0 + sc_ref[...]) + sh_ref[...]).astype(BF16)


def _resid_kernel(x_ref, y_ref, gpost_ref, gate_ref, gpre_ref, sc_ref, sh_ref, xo_ref, h_ref):
    x = x_ref[...] + gate_ref[...] * (_rms(y_ref[...]) * gpost_ref[...])
    xo_ref[...] = x
    h = _rms(x) * gpre_ref[...]
    h_ref[...] = (h * (1.0 + sc_ref[...]) + sh_ref[...]).astype(BF16)


def _resid_last_kernel(x_ref, y_ref, gpost_ref, gate_ref, xo_ref):
    xo_ref[...] = x_ref[...] + gate_ref[...] * (_rms(y_ref[...]) * gpost_ref[...])


def _row_spec(tr, D, i0=0):
    return pl.BlockSpec((tr, D), lambda i: (i0 + i, 0))


def _gain_spec(l, D):
    return pl.BlockSpec((None, 1, D), lambda i: (l, 0, 0))


def _mod_spec(l, j, tr, D, i0=0):
    return pl.BlockSpec((None, None, None, 1, D), lambda i: (l, j, ((i0 + i) * tr) // ROW_GROUP, 0, 0))


def _prenorm(xa, xb, gains, mods, l, j_sc, j_sh, tr=256):
    D = xa.shape[1]
    na = xa.shape[0] // tr
    M = xa.shape[0] + xb.shape[0]
    kern = functools.partial(_prenorm_kernel, na_tiles=na)
    return pl.pallas_call(
        kern,
        grid=(M // tr,),
        in_specs=[pl.BlockSpec((tr, D), lambda i: (jnp.minimum(i, na - 1), 0)),
                  pl.BlockSpec((tr, D), lambda i: (jnp.maximum(i - na, 0), 0)),
                  _gain_spec(l, D), _mod_spec(l, j_sc, tr, D), _mod_spec(l, j_sh, tr, D)],
        out_specs=[_row_spec(tr, D), _row_spec(tr, D)],
        out_shape=[jax.ShapeDtypeStruct((M, D), F32), jax.ShapeDtypeStruct((M, D), BF16)],
        compiler_params=_params("arbitrary"),
        name="prenorm",
    )(xa, xb, gains, mods, mods)


def _resid_last(x, y, gpost, mods, l, j_gate, row0, nrows, tr=256):
    D = x.shape[1]
    i0 = row0 // tr
    return pl.pallas_call(
        _resid_last_kernel,
        grid=(nrows // tr,),
        in_specs=[_row_spec(tr, D, i0), _row_spec(tr, D, i0), _gain_spec(l, D),
                  _mod_spec(l, j_gate, tr, D, i0)],
        out_specs=_row_spec(tr, D),
        out_shape=jax.ShapeDtypeStruct((nrows, D), F32),
        compiler_params=_params("arbitrary"),
        name="resid_last",
    )(x, y, gpost, mods)


def _resid(x, y, gpost, mods, l, j_gate, nxt, tr=256):
    M, D = x.shape
    ins = [_row_spec(tr, D), _row_spec(tr, D), _gain_spec(l, D), _mod_spec(l, j_gate, tr, D)]
    args = [x, y, gpost, mods]
    gpre, ln, j_sc, j_sh = nxt
    ins += [_gain_spec(ln, D), _mod_spec(ln, j_sc, tr, D), _mod_spec(ln, j_sh, tr, D)]
    args += [gpre, mods, mods]
    return pl.pallas_call(
        _resid_kernel,
        grid=(M // tr,),
        in_specs=ins,
        out_specs=[_row_spec(tr, D), _row_spec(tr, D)],
        out_shape=[jax.ShapeDtypeStruct((M, D), F32), jax.ShapeDtypeStruct((M, D), BF16)],
        compiler_params=_params("arbitrary"),
        name="resid",
    )(*args)


def _mm_kernel(x_ref, w_ref, o_ref):
    w = w_ref[...].astype(BF16)
    for r0 in range(0, x_ref.shape[0], MM_ROWS):
        rows = slice(r0, min(r0 + MM_ROWS, x_ref.shape[0]))
        o_ref[rows, :] = _bdot(x_ref[rows, :], w).astype(o_ref.dtype)


def _act_spec(tm, K, single_buffer, m0=0):
    if single_buffer:
        return pl.BlockSpec((tm, K), lambda m, n: (m0 + m, 0), pipeline_mode=pl.Buffered(1))
    return pl.BlockSpec((tm, K), lambda m, n: (m0 + m, 0))


def _matmul(x, w, l, out_dtype, tm, tn, name, col0=0, ncols=None, row0=0, nrows=None, single_buffer=False):
    K = x.shape[1]
    M = x.shape[0] - row0 if nrows is None else nrows
    N = w.shape[2] - col0 if ncols is None else ncols
    c0 = col0 // tn
    return pl.pallas_call(
        _mm_kernel,
        grid=(M // tm, N // tn),
        in_specs=[
            _act_spec(tm, K, single_buffer, row0 // tm),
            pl.BlockSpec((None, K, tn), lambda m, n: (l, 0, c0 + n)),
        ],
        out_specs=pl.BlockSpec((tm, tn), lambda m, n: (m, n)),
        out_shape=jax.ShapeDtypeStruct((M, N), out_dtype),
        compiler_params=_params("arbitrary", "arbitrary"),
        name=name,
    )(x, w)


def _proj_in_kernel(x_ref, w_ref, *rest, nq, nf, n_ctx_tiles, n_out):
    q_ref, k_ref, v_ref, kv_ref, f_ref = rest[-n_out:][:5]
    g_ref = rest[-1] if n_out == 6 else None
    m, n = pl.program_id(0), pl.program_id(1)
    is_ctx = m < n_ctx_tiles

    def project(o_ref):
        w = w_ref[...].astype(BF16)
        for r0 in range(0, x_ref.shape[0], MM_ROWS):
            y = _bdot(x_ref[r0:r0 + MM_ROWS, :], w)
            if o_ref.ndim == 3:
                seq = o_ref.shape[1]
                o_ref[r0 // seq:(r0 + MM_ROWS) // seq] = y.reshape(MM_ROWS // seq, seq, y.shape[1])
            else:
                o_ref[r0:r0 + MM_ROWS, :] = y.astype(o_ref.dtype)

    @pl.when(n < nq)
    def _():
        project(q_ref)

    @pl.when((n >= nq) & (n < 2 * nq) & is_ctx)
    def _():
        project(k_ref)

    @pl.when((n >= 2 * nq) & (n < 3 * nq) & is_ctx)
    def _():
        project(v_ref)

    @pl.when((n >= nq) & (n < 3 * nq) & jnp.logical_not(is_ctx))
    def _():
        project(kv_ref)

    @pl.when((n >= 3 * nq) & (n < 3 * nq + nf))
    def _():
        project(f_ref)

    if g_ref is not None:
        @pl.when(n >= 3 * nq + nf)
        def _():
            project(g_ref)


def _proj_in(h, w, l, caches, n_seq, seq, depth, n_f, n_cols, tm, tn):
    M, K = h.shape
    N = n_cols
    W = N_HEADS * HEAD_DIM
    nq, nf = W // tn, n_f // tn
    ng = N // tn - 3 * nq - nf
    mc = n_seq * seq // tm
    spt = tm // seq

    def cache_map(first):
        def index(m, n):
            ctx = m < mc
            return (jnp.where(ctx, m, mc - 1), l, 0, jnp.where(ctx, jnp.clip(n - first, 0, nq - 1), nq - 1))
        return index

    def kv_lat_map(m, n):
        return (jnp.maximum(m - mc, 0), jnp.where(m >= mc, jnp.clip(n - nq, 0, 2 * nq - 1), 0))

    in_specs = [
        pl.BlockSpec((tm, K), lambda m, n: (m, 0), pipeline_mode=pl.Buffered(1)),
        pl.BlockSpec((None, K, tn), lambda m, n: (l, 0, n)),
    ]
    args = [h, w]
    aliases = {}
    if caches is not None:
        in_specs += [pl.BlockSpec(memory_space=pl.ANY)] * 2
        args += list(caches)
        aliases = {2: 1, 3: 2}
    cache_shape = jax.ShapeDtypeStruct((n_seq, depth, seq, W), F32)
    out_specs = [
        pl.BlockSpec((tm, tn), lambda m, n: (m, jnp.clip(n, 0, nq - 1))),
        pl.BlockSpec((spt, None, seq, tn), cache_map(nq)),
        pl.BlockSpec((spt, None, seq, tn), cache_map(2 * nq)),
        pl.BlockSpec((tm, tn), kv_lat_map),
        pl.BlockSpec((tm, tn), lambda m, n: (m, jnp.clip(n - 3 * nq, 0, nf - 1))),
    ]
    out_shape = [
        jax.ShapeDtypeStruct((M, W), BF16), cache_shape, cache_shape,
        jax.ShapeDtypeStruct((M - mc * tm, 2 * W), BF16),
        jax.ShapeDtypeStruct((M, n_f), F32),
    ]
    if ng > 0:
        out_specs.append(pl.BlockSpec((tm, tn), lambda m, n: (m, jnp.clip(n - 3 * nq - nf, 0, ng - 1))))
        out_shape.append(jax.ShapeDtypeStruct((M, ng * tn), BF16))
    kern = functools.partial(_proj_in_kernel, nq=nq, nf=nf, n_ctx_tiles=mc, n_out=len(out_specs))
    outs = pl.pallas_call(
        kern,
        grid=(M // tm, N // tn),
        in_specs=in_specs,
        out_specs=out_specs,
        out_shape=out_shape,
        input_output_aliases=aliases,
        compiler_params=_params("arbitrary", "arbitrary"),
        name="proj_in",
    )(*args)
    q, new_k, new_v, kv_lat, fxbc = outs[:5]
    return q, (new_k, new_v), kv_lat, fxbc, (outs[5] if ng > 0 else None)


def _seq_pos(tm, n_ctx_tiles, seq_ctx, seq_lat):
    seq = jnp.where(pl.program_id(0) < n_ctx_tiles, seq_ctx, seq_lat)
    row = lax.broadcasted_iota(jnp.int32, (tm, 1), 0)
    return row & (seq - 1), seq


def _dwconv3(u, pos, seq, w_ref, b_ref):
    tm = u.shape[0]
    prev = jnp.where(pos == 0, 0.0, pltpu.roll(u, 1, 0))
    nxt = jnp.where(pos == seq - 1, 0.0, pltpu.roll(u, tm - 1, 0))
    return prev * w_ref[0:1, :] + u * w_ref[1:2, :] + nxt * w_ref[2:3, :] + b_ref[...]


def _ffn_up_kernel(h_ref, wg_ref, wv_ref, cw_ref, cb_ref, o_ref, *, n_ctx_tiles, seq_ctx, seq_lat):
    tm = h_ref.shape[0]
    wg = wg_ref[...].astype(BF16)
    wv = wv_ref[...].astype(BF16)
    seq = jnp.where(pl.program_id(0) < n_ctx_tiles, seq_ctx, seq_lat)
    bounds = [0] + [int(tm * f) for f in FFN_SPLITS] + [tm]
    gate = jnp.concatenate([_bdot(h_ref[r0:r1, :], wg) for r0, r1 in zip(bounds[:-1], bounds[1:])], axis=0)
    for r0, r1 in zip(bounds[:-1], bounds[1:]):
        val = _bdot(h_ref[r0:r1, :], wv)
        lo, hi = max(r0 - 8, 0), min(r1 + 8, tm)
        pos = (lo + lax.broadcasted_iota(jnp.int32, (hi - lo, 1), 0)) & (seq - 1)
        g = _dwconv3(gate[lo:hi], pos, seq, cw_ref, cb_ref)[r0 - lo:r1 - lo]
        o_ref[r0:r1, :] = (g * jax.nn.sigmoid(g) * val).astype(o_ref.dtype)


def _ffn_up(h, w_up, conv_w, conv_b, l, n_ctx_rows, seq_ctx, seq_lat, tm=PROJ_TM, tn=PROJ_TN):
    M, K = h.shape
    L, _, N2 = w_up.shape
    F = N2 // 2
    nb = F // tn
    kern = functools.partial(_ffn_up_kernel, n_ctx_tiles=n_ctx_rows // tm, seq_ctx=seq_ctx, seq_lat=seq_lat)
    return pl.pallas_call(
        kern,
        grid=(M // tm, nb),
        in_specs=[
            _act_spec(tm, K, True),
            pl.BlockSpec((None, K, tn), lambda m, n: (l, 0, n)),
            pl.BlockSpec((None, K, tn), lambda m, n: (l, 0, nb + n)),
            pl.BlockSpec((None, 3, tn), lambda m, n: (l, 0, n)),
            pl.BlockSpec((None, 1, tn), lambda m, n: (l, 0, n)),
        ],
        out_specs=pl.BlockSpec((tm, tn), lambda m, n: (m, n)),
        out_shape=jax.ShapeDtypeStruct((M, F), BF16),
        compiler_params=_params("arbitrary", "arbitrary"),
        name="ffn_up",
    )(h, w_up, w_up, conv_w, conv_b.reshape(L, 1, F))


def _sconv_kernel(zx_ref, zb_ref, zc_ref, cw_ref, cb_ref, o_ref, *, n_ctx_tiles, seq_ctx, seq_lat):
    u = zc_ref[...] * zx_ref[...]
    pos, seq = _seq_pos(u.shape[0], n_ctx_tiles, seq_ctx, seq_lat)
    o_ref[...] = (zb_ref[...] * _dwconv3(u, pos, seq, cw_ref, cb_ref)).astype(o_ref.dtype)


def _sconv(z, col0, width, conv_w, conv_b, l, n_ctx_rows, seq_ctx, seq_lat, tm=1024, tn=512):
    M = z.shape[0]
    L = conv_w.shape[0]
    kern = functools.partial(_sconv_kernel, n_ctx_tiles=n_ctx_rows // tm, seq_ctx=seq_ctx, seq_lat=seq_lat)
    cb0 = col0 // tn
    wb = width // tn
    return pl.pallas_call(
        kern,
        grid=(M // tm, wb),
        in_specs=[
            pl.BlockSpec((tm, tn), lambda m, n: (m, cb0 + n)),
            pl.BlockSpec((tm, tn), lambda m, n: (m, cb0 + wb + n)),
            pl.BlockSpec((tm, tn), lambda m, n: (m, cb0 + 2 * wb + n)),
            pl.BlockSpec((None, 3, tn), lambda m, n: (l, 0, n)),
            pl.BlockSpec((None, 1, tn), lambda m, n: (l, 0, n)),
        ],
        out_specs=pl.BlockSpec((tm, tn), lambda m, n: (m, n)),
        out_shape=jax.ShapeDtypeStruct((M, width), BF16),
        compiler_params=_params("arbitrary", "arbitrary"),
        name="sconv",
    )(z, z, z, conv_w, conv_b.reshape(L, 1, width))


def _merge_kernel(a_ref, f_ref, c_ref, wa_ref, wf_ref, wc_ref, ga_ref, gf_ref, gc_ref, o_ref):
    wa = wa_ref[...].astype(BF16)
    wf = wf_ref[...].astype(BF16)
    wc = wc_ref[...].astype(BF16)
    for r0 in range(0, o_ref.shape[0], MERGE_CHUNK):
        rows = slice(r0, r0 + MERGE_CHUNK)

        def gated(x_ref, w, g_ref):
            return jax.nn.sigmoid(g_ref[rows, :].astype(F32)) * _bdot(x_ref[rows, :], w)

        o = gated(a_ref, wa, ga_ref) + gated(f_ref, wf, gf_ref) + gated(c_ref, wc, gc_ref)
        o_ref[rows, :] = o.astype(o_ref.dtype)


def _merge(att, fr, cv, w_na_out, w_fnet_out, w_conv_out, z, l, tm=PROJ_TM, tn=PROJ_TN):
    M = att.shape[0]
    D = w_na_out.shape[2]
    gb = D // tn

    def act(a):
        return _act_spec(tm, a.shape[1], False)

    def wgt(w):
        return pl.BlockSpec((None, w.shape[1], tn), lambda m, n: (l, 0, n))

    def gate(j):
        return pl.BlockSpec((tm, tn), lambda m, n: (m, j * gb + n))

    return pl.pallas_call(
        _merge_kernel,
        grid=(M // tm, D // tn),
        in_specs=[act(att), act(fr), act(cv), wgt(w_na_out), wgt(w_fnet_out), wgt(w_conv_out),
                  gate(0), gate(1), gate(2)],
        out_specs=pl.BlockSpec((tm, tn), lambda m, n: (m, n)),
        out_shape=jax.ShapeDtypeStruct((M, D), BF16),
        compiler_params=_params("arbitrary", "arbitrary"),
        name="merge",
    )(att, fr, cv, w_na_out, w_fnet_out, w_conv_out, z, z, z)


def _softmax_pv(s_list, v_list):
    m = functools.reduce(jnp.maximum, [jnp.max(s, axis=-1, keepdims=True) for s in s_list])
    e_list = [jnp.exp(s - m) for s in s_list]
    denom = functools.reduce(jnp.add, [jnp.sum(e, axis=-1, keepdims=True) for e in e_list])
    outs = [_bdot(e.astype(BF16), v) for e, v in zip(e_list, v_list)]
    return functools.reduce(jnp.add, outs) * (1.0 / denom)


def _qk(q, k):
    return lax.dot_general(q, k, (((1,), (1,)), ((), ())), preferred_element_type=F32)


def _ctx_attn_kernel(q_ref, k_ref, v_ref, o_ref, *, scale):
    for h in range(N_HEADS):
        sl = slice(h * HEAD_DIM, (h + 1) * HEAD_DIM)
        q = q_ref[:, sl].astype(BF16)
        k = k_ref[:, sl].astype(BF16)
        v = v_ref[:, sl].astype(BF16)
        s = _qk(q, k) * scale
        o_ref[:, sl] = _softmax_pv([s], [v]).astype(o_ref.dtype)


def _ctx_attention(q, new_k, new_v, l, out_rows):
    n_seq, _, seq, W = new_k.shape
    kern = functools.partial(_ctx_attn_kernel, scale=HEAD_DIM ** -0.5)
    kv_spec = pl.BlockSpec((None, None, seq, W), lambda b: (b, l, 0, 0))
    return pl.pallas_call(
        kern,
        grid=(n_seq,),
        in_specs=[pl.BlockSpec((seq, W), lambda b: (b, 0)), kv_spec, kv_spec],
        out_specs=pl.BlockSpec((seq, W), lambda b: (b, 0)),
        out_shape=jax.ShapeDtypeStruct((out_rows, W), BF16),
        compiler_params=_params("arbitrary"),
        name="ctx_attention",
    )(q, new_k, new_v)


def _window_starts(rows):
    kr = min(NA_ROWS, rows)
    return [min(max(r - kr // 2, 0), rows - kr) for r in range(rows)], kr


def _row_groups(rows):
    starts, kr = _window_starts(rows)
    groups = []
    for r0 in range(0, rows, NA_GROUP_ROWS):
        r1 = min(r0 + NA_GROUP_ROWS, rows)
        groups.append((r0, r1, min(starts[r0:r1]), max(starts[r0:r1]) + kr))
    return groups


def _na_attn_kernel(att_ref, q_ref, k_ref, v_ref, kc_ref, vc_ref, *rest, scale, rows):
    bias_refs, o_ref = rest[:-1], rest[-1]
    q = q_ref[...].astype(BF16)
    k = k_ref[...].astype(BF16)
    v = v_ref[...].astype(BF16)
    vc = vc_ref[...].astype(BF16)
    s_ctx = _qk(q, kc_ref[...].astype(BF16)) * scale
    for (r0, r1, k0, k1), bias_ref in zip(_row_groups(rows), bias_refs):
        qs = slice(r0 * GRID_W, r1 * GRID_W)
        ks = slice(k0 * GRID_W, k1 * GRID_W)
        s_lat = _qk(q[qs], k[ks]) * scale + bias_ref[...]
        o_ref[qs, :] = _softmax_pv([s_lat, s_ctx[qs]], [v[ks], vc]).astype(o_ref.dtype)


def _na_attention(att, q, kv, row0, n_seq, seq, cache_k, cache_v, biases, l):
    P = cache_k.shape[2]
    r0 = row0 // seq
    rows = seq // GRID_W
    kern = functools.partial(_na_attn_kernel, scale=HEAD_DIM ** -0.5, rows=rows)

    def kvspec(j):
        return pl.BlockSpec((seq, HEAD_DIM), lambda h, b: (b, j * N_HEADS + h))

    cspec = pl.BlockSpec((None, None, P, HEAD_DIM), lambda h, b: (b, l, 0, h))
    bspecs = [pl.BlockSpec((None, None) + bias.shape[2:], lambda h, b: (l, h, 0, 0)) for bias in biases]
    return pl.pallas_call(
        kern,
        grid=(N_HEADS, n_seq),
        in_specs=[pl.BlockSpec(memory_space=pl.ANY),
                  pl.BlockSpec((seq, HEAD_DIM), lambda h, b: (r0 + b, h)),
                  kvspec(0), kvspec(1), cspec, cspec] + bspecs,
        out_specs=pl.BlockSpec((seq, HEAD_DIM), lambda h, b: (r0 + b, h)),
        out_shape=jax.ShapeDtypeStruct(att.shape, att.dtype),
        input_output_aliases={0: 0},
        compiler_params=_params("arbitrary", "arbitrary"),
        name="na_attention",
    )(att, q, kv, kv, cache_k, cache_v, *biases)


def _na_bias(rpb, rows):
    starts, kr = _window_starts(rows)
    col = np.arange(GRID_W)
    cs = np.clip(col - NA_COLS // 2, 0, GRID_W - NA_COLS)
    col_ok = (col[None, :] >= cs[:, None]) & (col[None, :] < cs[:, None] + NA_COLS)
    col_idx = np.clip(col[None, :] - col[:, None] + NA_COLS - 1, 0, 2 * NA_COLS - 2)
    onehot = (col_idx[None] == np.arange(2 * NA_COLS - 1)[:, None, None]) & col_ok[None]
    e = jnp.einsum('lhrd,dqk->lhqrk', rpb.astype(F32), jnp.asarray(onehot, F32),
                   precision=lax.Precision.HIGHEST)
    e = jnp.where(jnp.asarray(col_ok)[:, None, :], e, NEG_INF)
    biases = []
    for r0, r1, k0, k1 in _row_groups(rows):
        per_row = []
        for r in range(r0, r1):
            d0 = starts[r] - r + NA_ROWS - 1
            band = e[:, :, :, d0:d0 + kr]
            band = band.reshape(band.shape[:3] + (kr * GRID_W,))
            pad = ((starts[r] - k0) * GRID_W, (k1 - starts[r] - kr) * GRID_W)
            per_row.append(jnp.pad(band, ((0, 0), (0, 0), (0, 0), pad), constant_values=NEG_INF))
        biases.append(jnp.concatenate(per_row, axis=2))
    return biases


def _split_bf16(x):
    hi = x.astype(BF16)
    return hi, (x - hi.astype(F32)).astype(BF16)


def _dot_split(a, b):
    (ah, al), (bh, bl) = a, b
    return _bdot(ah, bh) + _bdot(ah, bl) + _bdot(al, bh)


def _fnet_kernel(x_ref, wch_ref, wcl_ref, wth_ref, wtl_ref, *rest):
    o_ref = rest[-1]
    t = _dot_split(_split_bf16(x_ref[...]), (wch_ref[...], wcl_ref[...]))
    g = t.shape[1] // 2
    u = jnp.concatenate([t[:, :g], t[:, g:]], axis=0)
    y = _dot_split((wth_ref[...], wtl_ref[...]), _split_bf16(u))
    o_ref[...] = y.astype(o_ref.dtype)


def _dft_mats(n):
    k = np.arange(n, dtype=np.int64)
    ang = 2.0 * np.pi * ((k[:, None] * k[None, :]) % n) / n
    s = 1.0 / np.sqrt(n)
    return np.cos(ang) * s, np.sin(ang) * s


def _split_const(w):
    w = jnp.asarray(w, F32)
    return _split_bf16(w)


def _fnet(z, col0, groups, gdim, row0, n_seq, seq, prev=None):
    ct, st = _dft_mats(seq)
    cc, sc = _dft_mats(gdim)
    wch, wcl = _split_const(np.concatenate([cc, sc], axis=1))
    wth, wtl = _split_const(np.concatenate([ct, -st], axis=1))
    r0 = row0 // seq
    c0 = col0 // gdim

    def const(w):
        return pl.BlockSpec(w.shape, lambda b, g: (0, 0))

    in_specs = [pl.BlockSpec((seq, gdim), lambda b, g: (r0 + b, c0 + g)),
                const(wch), const(wcl), const(wth), const(wtl)]
    args = [z, wch, wcl, wth, wtl]
    aliases = {}
    if prev is not None:
        in_specs.append(pl.BlockSpec(memory_space=pl.ANY))
        args.append(prev)
        aliases = {5: 0}
    return pl.pallas_call(
        _fnet_kernel,
        grid=(n_seq, groups),
        in_specs=in_specs,
        out_specs=pl.BlockSpec((seq, gdim), lambda b, g: (r0 + b, g)),
        out_shape=jax.ShapeDtypeStruct((z.shape[0], groups * gdim), BF16),
        input_output_aliases=aliases,
        compiler_params=_params("arbitrary", "arbitrary"),
        name="fnet",
    )(*args)


def kernel(x_prompt, x_sample, cache_k, cache_v, c, c_ctx, w_mod, b_mod, g_pre1, g_post1, g_pre2, g_post2,
           w_in, rpb, w_na_out, w_fnet_out, conv_w, conv_b, w_conv_out, w_o, w_up, ffn_conv_w, ffn_conv_b,
           w_down):
    B, S, D = x_prompt.shape
    Bd, T, _ = x_sample.shape
    L = w_mod.shape[0]
    P = cache_k.shape[2]
    W = N_HEADS * HEAD_DIM
    n_ctx = B * S
    n_lat = Bd * T
    fw = w_fnet_out.shape[1]
    cwid = w_conv_out.shape[1]
    fgroups = 4
    assert T == ROW_GROUP and n_ctx % ROW_GROUP == 0 and Bd + 1 <= 8

    c_rows = jnp.zeros((8, D), F32).at[:Bd].set(c).at[Bd].set(c_ctx)
    mod = _modulation(c_rows, w_mod, b_mod)
    group_row = np.concatenate([np.full(n_ctx // ROW_GROUP, Bd), np.arange(Bd)])
    mods = mod.reshape(L, 8, 6, D)[:, group_row]
    mods = mods.transpose(0, 2, 1, 3)[:, :, :, None, :]

    gains = [g.reshape(L, 1, D) for g in (g_pre1, g_post1, g_pre2, g_post2)]
    ck = cache_k.reshape(Bd, L, P, W)
    cv_cache = cache_v.reshape(Bd, L, P, W)

    col_f = 3 * W
    col_gate = col_f + fw + 3 * cwid
    biases = _na_bias(rpb, T // GRID_W)

    new_kv = None
    x, h = _prenorm(x_prompt.reshape(n_ctx, D), x_sample.reshape(n_lat, D), gains[0], mods, 0, 1, 0)
    for l in range(L):
        tm, tn = PROJ_TM, PROJ_TN
        q, new_kv, kv_lat, fxbc, _ = _proj_in(h, w_in, l, new_kv, B, S, L, col_gate - col_f, col_gate, tm, tn)
        zg = _matmul(h, w_in, l, BF16, tm, 2 * tn, "proj_gates", col_gate, single_buffer=True)
        new_k, new_v = new_kv
        att = _ctx_attention(q, new_k, new_v, l, n_ctx + n_lat)
        att = _na_attention(att, q, kv_lat, n_ctx, Bd, T, ck, cv_cache, biases, l)
        fr = _fnet(fxbc, 0, fgroups, fw // fgroups, 0, B, S)
        fr = _fnet(fxbc, 0, fgroups, fw // fgroups, n_ctx, Bd, T, prev=fr)
        cv = _sconv(fxbc, fw, cwid, conv_w, conv_b, l, n_ctx, S, T)
        merged = _merge(att, fr, cv, w_na_out, w_fnet_out, w_conv_out, zg, l)
        y = _matmul(merged, w_o, l, F32, tm, tn, "proj_o")
        x, h = _resid(x, y, gains[1], mods, l, 2, (gains[2], l, 4, 3))
        act = _ffn_up(h, w_up, ffn_conv_w, ffn_conv_b, l, n_ctx, S, T)
        y = _matmul(act, w_down, l, F32, 1024, 256, "ffn_down", single_buffer=True)
        if l + 1 < L:
            x, h = _resid(x, y, gains[3], mods, l, 5, (gains[0], l + 1, 1, 0))

    y_prompt = _resid_last(x, y, gains[3], mods, L - 1, 5, 0, n_ctx).reshape(B, S, D)
    y_sample = _resid_last(x, y, gains[3], mods, L - 1, 5, n_ctx, n_lat).reshape(Bd, T, D)
    kv_shape = (B, L, S, N_HEADS, HEAD_DIM)
    return y_prompt, y_sample, new_k.reshape(kv_shape), new_v.reshape(kv_shape)
```

```python
import functools

import numpy as np
import jax
import jax.numpy as jnp
from jax import lax
from jax.experimental import pallas as pl
from jax.experimental.pallas import tpu as pltpu

F32 = jnp.float32
BF16 = jnp.bfloat16

EPS = 1e-6
NEG_INF = -1e30
GRID_W = 64
NA_ROWS = 8
NA_COLS = 16
N_HEADS = 16
HEAD_DIM = 128
MERGE_CHUNK = 256
FFN_SPLITS = (0.4375, 0.875)
NA_GROUP_ROWS = 4

VMEM_LIMIT_BYTES = 58 * 1024 * 1024
ROW_GROUP = 1024
PROJ_TM, PROJ_TN = 2048, 256
MM_ROWS = 1024


def _params(*sem):
    return pltpu.CompilerParams(dimension_semantics=sem, vmem_limit_bytes=VMEM_LIMIT_BYTES)


def _bdot(a, b):
    return jnp.dot(a, b, preferred_element_type=F32)


def _mod_kernel(c_ref, w_ref, b_ref, o_ref):
    c = c_ref[...]
    s = (c * jax.nn.sigmoid(c)).astype(BF16)
    o_ref[...] = _bdot(s, w_ref[...].astype(BF16)) + b_ref[...]


def _modulation(c_rows, w_mod, b_mod, tn=1024):
    L, D, N = w_mod.shape
    return pl.pallas_call(
        _mod_kernel,
        grid=(L, N // tn),
        in_specs=[
            pl.BlockSpec((8, D), lambda l, n: (0, 0)),
            pl.BlockSpec((None, D, tn), lambda l, n: (l, 0, n)),
            pl.BlockSpec((None, 1, tn), lambda l, n: (l, 0, n)),
        ],
        out_specs=pl.BlockSpec((None, 8, tn), lambda l, n: (l, 0, n)),
        out_shape=jax.ShapeDtypeStruct((L, 8, N), F32),
        compiler_params=_params("arbitrary", "arbitrary"),
        name="modulation",
    )(c_rows, w_mod, b_mod.reshape(L, 1, N))


def _rms(x):
    return x * lax.rsqrt(jnp.mean(x * x, axis=-1, keepdims=True) + EPS)


def _prenorm_kernel(xa_ref, xb_ref, g_ref, sc_ref, sh_ref, x_ref, h_ref, *, na_tiles):
    x = jnp.where(pl.program_id(0) < na_tiles, xa_ref[...], xb_ref[...])
    x_ref[...] = x
    h = _rms(x) * g_ref[...]
    h_ref[...] = (h * (1.0 + sc_ref[...]) + sh_ref[...]).astype(BF16)


def _resid_kernel(x_ref, y_ref, gpost_ref, gate_ref, gpre_ref, sc_ref, sh_ref, xo_ref, h_ref):
    x = x_ref[...] + gate_ref[...] * (_rms(y_ref[...]) * gpost_ref[...])
    xo_ref[...] = x
    h = _rms(x) * gpre_ref[...]
    h_ref[...] = (h * (1.0 + sc_ref[...]) + sh_ref[...]).astype(BF16)


def _resid_last_kernel(x_ref, y_ref, gpost_ref, gate_ref, xo_ref):
    xo_ref[...] = x_ref[...] + gate_ref[...] * (_rms(y_ref[...]) * gpost_ref[...])


def _row_spec(tr, D, i0=0):
    return pl.BlockSpec((tr, D), lambda i: (i0 + i, 0))


def _gain_spec(l, D):
    return pl.BlockSpec((None, 1, D), lambda i: (l, 0, 0))


def _mod_spec(l, j, tr, D, i0=0):
    return pl.BlockSpec((None, None, None, 1, D), lambda i: (l, j, ((i0 + i) * tr) // ROW_GROUP, 0, 0))


def _prenorm(xa, xb, gains, mods, l, j_sc, j_sh, tr=256):
    D = xa.shape[1]
    na = xa.shape[0] // tr
    M = xa.shape[0] + xb.shape[0]
    kern = functools.partial(_prenorm_kernel, na_tiles=na)
    return pl.pallas_call(
        kern,
        grid=(M // tr,),
        in_specs=[pl.BlockSpec((tr, D), lambda i: (jnp.minimum(i, na - 1), 0)),
                  pl.BlockSpec((tr, D), lambda i: (jnp.maximum(i - na, 0), 0)),
                  _gain_spec(l, D), _mod_spec(l, j_sc, tr, D), _mod_spec(l, j_sh, tr, D)],
        out_specs=[_row_spec(tr, D), _row_spec(tr, D)],
        out_shape=[jax.ShapeDtypeStruct((M, D), F32), jax.ShapeDtypeStruct((M, D), BF16)],
        compiler_params=_params("arbitrary"),
        name="prenorm",
    )(xa, xb, gains, mods, mods)


def _resid_last(x, y, gpost, mods, l, j_gate, row0, nrows, tr=256):
    D = x.shape[1]
    i0 = row0 // tr
    return pl.pallas_call(
        _resid_last_kernel,
        grid=(nrows // tr,),
        in_specs=[_row_spec(tr, D, i0), _row_spec(tr, D, i0), _gain_spec(l, D),
                  _mod_spec(l, j_gate, tr, D, i0)],
        out_specs=_row_spec(tr, D),
        out_shape=jax.ShapeDtypeStruct((nrows, D), F32),
        compiler_params=_params("arbitrary"),
        name="resid_last",
    )(x, y, gpost, mods)


def _resid(x, y, gpost, mods, l, j_gate, nxt, tr=256):
    M, D = x.shape
    ins = [_row_spec(tr, D), _row_spec(tr, D), _gain_spec(l, D), _mod_spec(l, j_gate, tr, D)]
    args = [x, y, gpost, mods]
    gpre, ln, j_sc, j_sh = nxt
    ins += [_gain_spec(ln, D), _mod_spec(ln, j_sc, tr, D), _mod_spec(ln, j_sh, tr, D)]
    args += [gpre, mods, mods]
    return pl.pallas_call(
        _resid_kernel,
        grid=(M // tr,),
        in_specs=ins,
        out_specs=[_row_spec(tr, D), _row_spec(tr, D)],
        out_shape=[jax.ShapeDtypeStruct((M, D), F32), jax.ShapeDtypeStruct((M, D), BF16)],
        compiler_params=_params("arbitrary"),
        name="resid",
    )(*args)


def _mm_kernel(x_ref, w_ref, o_ref):
    w = w_ref[...].astype(BF16)
    for r0 in range(0, x_ref.shape[0], MM_ROWS):
        rows = slice(r0, min(r0 + MM_ROWS, x_ref.shape[0]))
        o_ref[rows, :] = _bdot(x_ref[rows, :], w).astype(o_ref.dtype)


def _act_spec(tm, K, single_buffer, m0=0):
    if single_buffer:
        return pl.BlockSpec((tm, K), lambda m, n: (m0 + m, 0), pipeline_mode=pl.Buffered(1))
    return pl.BlockSpec((tm, K), lambda m, n: (m0 + m, 0))


def _matmul(x, w, l, out_dtype, tm, tn, name, col0=0, ncols=None, row0=0, nrows=None, single_buffer=False):
    K = x.shape[1]
    M = x.shape[0] - row0 if nrows is None else nrows
    N = w.shape[2] - col0 if ncols is None else ncols
    c0 = col0 // tn
    return pl.pallas_call(
        _mm_kernel,
        grid=(M // tm, N // tn),
        in_specs=[
            _act_spec(tm, K, single_buffer, row0 // tm),
            pl.BlockSpec((None, K, tn), lambda m, n: (l, 0, c0 + n)),
        ],
        out_specs=pl.BlockSpec((tm, tn), lambda m, n: (m, n)),
        out_shape=jax.ShapeDtypeStruct((M, N), out_dtype),
        compiler_params=_params("arbitrary", "arbitrary"),
        name=name,
    )(x, w)


def _proj_in_kernel(x_ref, w_ref, *rest, nq, nf, n_ctx_tiles, n_out):
    q_ref, k_ref, v_ref, kv_ref, f_ref = rest[-n_out:][:5]
    g_ref = rest[-1] if n_out == 6 else None
    m, n = pl.program_id(0), pl.program_id(1)
    is_ctx = m < n_ctx_tiles

    def project(o_ref):
        w = w_ref[...].astype(BF16)
        for r0 in range(0, x_ref.shape[0], MM_ROWS):
            y = _bdot(x_ref[r0:r0 + MM_ROWS, :], w)
            if o_ref.ndim == 3:
                seq = o_ref.shape[1]
                o_ref[r0 // seq:(r0 + MM_ROWS) // seq] = y.reshape(MM_ROWS // seq, seq, y.shape[1])
            else:
                o_ref[r0:r0 + MM_ROWS, :] = y.astype(o_ref.dtype)

    @pl.when(n < nq)
    def _():
        project(q_ref)

    @pl.when((n >= nq) & (n < 2 * nq) & is_ctx)
    def _():
        project(k_ref)

    @pl.when((n >= 2 * nq) & (n < 3 * nq) & is_ctx)
    def _():
        project(v_ref)

    @pl.when((n >= nq) & (n < 3 * nq) & jnp.logical_not(is_ctx))
    def _():
        project(kv_ref)

    @pl.when((n >= 3 * nq) & (n < 3 * nq + nf))
    def _():
        project(f_ref)

    if g_ref is not None:
        @pl.when(n >= 3 * nq + nf)
        def _():
            project(g_ref)


def _proj_in(h, w, l, caches, n_seq, seq, depth, n_f, n_cols, tm, tn):
    M, K = h.shape
    N = n_cols
    W = N_HEADS * HEAD_DIM
    nq, nf = W // tn, n_f // tn
    ng = N // tn - 3 * nq - nf
    mc = n_seq * seq // tm
    spt = tm // seq

    def cache_map(first):
        def index(m, n):
            ctx = m < mc
            return (jnp.where(ctx, m, mc - 1), l, 0, jnp.where(ctx, jnp.clip(n - first, 0, nq - 1), nq - 1))
        return index

    def kv_lat_map(m, n):
        return (jnp.maximum(m - mc, 0), jnp.where(m >= mc, jnp.clip(n - nq, 0, 2 * nq - 1), 0))

    in_specs = [
        pl.BlockSpec((tm, K), lambda m, n: (m, 0), pipeline_mode=pl.Buffered(1)),
        pl.BlockSpec((None, K, tn), lambda m, n: (l, 0, n)),
    ]
    args = [h, w]
    aliases = {}
    if caches is not None:
        in_specs += [pl.BlockSpec(memory_space=pl.ANY)] * 2
        args += list(caches)
        aliases = {2: 1, 3: 2}
    cache_shape = jax.ShapeDtypeStruct((n_seq, depth, seq, W), F32)
    out_specs = [
        pl.BlockSpec((tm, tn), lambda m, n: (m, jnp.clip(n, 0, nq - 1))),
        pl.BlockSpec((spt, None, seq, tn), cache_map(nq)),
        pl.BlockSpec((spt, None, seq, tn), cache_map(2 * nq)),
        pl.BlockSpec((tm, tn), kv_lat_map),
        pl.BlockSpec((tm, tn), lambda m, n: (m, jnp.clip(n - 3 * nq, 0, nf - 1))),
    ]
    out_shape = [
        jax.ShapeDtypeStruct((M, W), BF16), cache_shape, cache_shape,
        jax.ShapeDtypeStruct((M - mc * tm, 2 * W), BF16),
        jax.ShapeDtypeStruct((M, n_f), F32),
    ]
    if ng > 0:
        out_specs.append(pl.BlockSpec((tm, tn), lambda m, n: (m, jnp.clip(n - 3 * nq - nf, 0, ng - 1))))
        out_shape.append(jax.ShapeDtypeStruct((M, ng * tn), BF16))
    kern = functools.partial(_proj_in_kernel, nq=nq, nf=nf, n_ctx_tiles=mc, n_out=len(out_specs))
    outs = pl.pallas_call(
        kern,
        grid=(M // tm, N // tn),
        in_specs=in_specs,
        out_specs=out_specs,
        out_shape=out_shape,
        input_output_aliases=aliases,
        compiler_params=_params("arbitrary", "arbitrary"),
        name="proj_in",
    )(*args)
    q, new_k, new_v, kv_lat, fxbc = outs[:5]
    return q, (new_k, new_v), kv_lat, fxbc, (outs[5] if ng > 0 else None)


def _seq_pos(tm, n_ctx_tiles, seq_ctx, seq_lat):
    seq = jnp.where(pl.program_id(0) < n_ctx_tiles, seq_ctx, seq_lat)
    row = lax.broadcasted_iota(jnp.int32, (tm, 1), 0)
    return row & (seq - 1), seq


def _dwconv3(u, pos, seq, w_ref, b_ref):
    tm = u.shape[0]
    prev = jnp.where(pos == 0, 0.0, pltpu.roll(u, 1, 0))
    nxt = jnp.where(pos == seq - 1, 0.0, pltpu.roll(u, tm - 1, 0))
    return prev * w_ref[0:1, :] + u * w_ref[1:2, :] + nxt * w_ref[2:3, :] + b_ref[...]


def _ffn_up_kernel(h_ref, wg_ref, wv_ref, cw_ref, cb_ref, o_ref, *, n_ctx_tiles, seq_ctx, seq_lat):
    tm = h_ref.shape[0]
    wg = wg_ref[...].astype(BF16)
    wv = wv_ref[...].astype(BF16)
    seq = jnp.where(pl.program_id(0) < n_ctx_tiles, seq_ctx, seq_lat)
    bounds = [0] + [int(tm * f) for f in FFN_SPLITS] + [tm]
    gate = jnp.concatenate([_bdot(h_ref[r0:r1, :], wg) for r0, r1 in zip(bounds[:-1], bounds[1:])], axis=0)
    for r0, r1 in zip(bounds[:-1], bounds[1:]):
        val = _bdot(h_ref[r0:r1, :], wv)
        lo, hi = max(r0 - 8, 0), min(r1 + 8, tm)
        pos = (lo + lax.broadcasted_iota(jnp.int32, (hi - lo, 1), 0)) & (seq - 1)
        g = _dwconv3(gate[lo:hi], pos, seq, cw_ref, cb_ref)[r0 - lo:r1 - lo]
        o_ref[r0:r1, :] = (g * jax.nn.sigmoid(g) * val).astype(o_ref.dtype)


def _ffn_up(h, w_up, conv_w, conv_b, l, n_ctx_rows, seq_ctx, seq_lat, tm=PROJ_TM, tn=PROJ_TN):
    M, K = h.shape
    L, _, N2 = w_up.shape
    F = N2 // 2
    nb = F // tn
    kern = functools.partial(_ffn_up_kernel, n_ctx_tiles=n_ctx_rows // tm, seq_ctx=seq_ctx, seq_lat=seq_lat)
    return pl.pallas_call(
        kern,
        grid=(M // tm, nb),
        in_specs=[
            _act_spec(tm, K, True),
            pl.BlockSpec((None, K, tn), lambda m, n: (l, 0, n)),
            pl.BlockSpec((None, K, tn), lambda m, n: (l, 0, nb + n)),
            pl.BlockSpec((None, 3, tn), lambda m, n: (l, 0, n)),
            pl.BlockSpec((None, 1, tn), lambda m, n: (l, 0, n)),
        ],
        out_specs=pl.BlockSpec((tm, tn), lambda m, n: (m, n)),
        out_shape=jax.ShapeDtypeStruct((M, F), BF16),
        compiler_params=_params("arbitrary", "arbitrary"),
        name="ffn_up",
    )(h, w_up, w_up, conv_w, conv_b.reshape(L, 1, F))


def _sconv_kernel(zx_ref, zb_ref, zc_ref, cw_ref, cb_ref, o_ref, *, n_ctx_tiles, seq_ctx, seq_lat):
    u = zc_ref[...] * zx_ref[...]
    pos, seq = _seq_pos(u.shape[0], n_ctx_tiles, seq_ctx, seq_lat)
    o_ref[...] = (zb_ref[...] * _dwconv3(u, pos, seq, cw_ref, cb_ref)).astype(o_ref.dtype)


def _sconv(z, col0, width, conv_w, conv_b, l, n_ctx_rows, seq_ctx, seq_lat, tm=1024, tn=512):
    M = z.shape[0]
    L = conv_w.shape[0]
    kern = functools.partial(_sconv_kernel, n_ctx_tiles=n_ctx_rows // tm, seq_ctx=seq_ctx, seq_lat=seq_lat)
    cb0 = col0 // tn
    wb = width // tn
    return pl.pallas_call(
        kern,
        grid=(M // tm, wb),
        in_specs=[
            pl.BlockSpec((tm, tn), lambda m, n: (m, cb0 + n)),
            pl.BlockSpec((tm, tn), lambda m, n: (m, cb0 + wb + n)),
            pl.BlockSpec((tm, tn), lambda m, n: (m, cb0 + 2 * wb + n)),
            pl.BlockSpec((None, 3, tn), lambda m, n: (l, 0, n)),
            pl.BlockSpec((None, 1, tn), lambda m, n: (l, 0, n)),
        ],
        out_specs=pl.BlockSpec((tm, tn), lambda m, n: (m, n)),
        out_shape=jax.ShapeDtypeStruct((M, width), BF16),
        compiler_params=_params("arbitrary", "arbitrary"),
        name="sconv",
    )(z, z, z, conv_w, conv_b.reshape(L, 1, width))


def _merge_kernel(a_ref, f_ref, c_ref, wa_ref, wf_ref, wc_ref, ga_ref, gf_ref, gc_ref, o_ref):
    wa = wa_ref[...].astype(BF16)
    wf = wf_ref[...].astype(BF16)
    wc = wc_ref[...].astype(BF16)
    for r0 in range(0, o_ref.shape[0], MERGE_CHUNK):
        rows = slice(r0, r0 + MERGE_CHUNK)

        def gated(x_ref, w, g_ref):
            return jax.nn.sigmoid(g_ref[rows, :].astype(F32)) * _bdot(x_ref[rows, :], w)

        o = gated(a_ref, wa, ga_ref) + gated(f_ref, wf, gf_ref) + gated(c_ref, wc, gc_ref)
        o_ref[rows, :] = o.astype(o_ref.dtype)


def _merge(att, fr, cv, w_na_out, w_fnet_out, w_conv_out, z, l, tm=PROJ_TM, tn=PROJ_TN):
    M = att.shape[0]
    D = w_na_out.shape[2]
    gb = D // tn

    def act(a):
        return _act_spec(tm, a.shape[1], False)

    def wgt(w):
        return pl.BlockSpec((None, w.shape[1], tn), lambda m, n: (l, 0, n))

    def gate(j):
        return pl.BlockSpec((tm, tn), lambda m, n: (m, j * gb + n))

    return pl.pallas_call(
        _merge_kernel,
        grid=(M // tm, D // tn),
        in_specs=[act(att), act(fr), act(cv), wgt(w_na_out), wgt(w_fnet_out), wgt(w_conv_out),
                  gate(0), gate(1), gate(2)],
        out_specs=pl.BlockSpec((tm, tn), lambda m, n: (m, n)),
        out_shape=jax.ShapeDtypeStruct((M, D), BF16),
        compiler_params=_params("arbitrary", "arbitrary"),
        name="merge",
    )(att, fr, cv, w_na_out, w_fnet_out, w_conv_out, z, z, z)


def _softmax_pv(s_list, v_list):
    m = functools.reduce(jnp.maximum, [jnp.max(s, axis=-1, keepdims=True) for s in s_list])
    e_list = [jnp.exp(s - m) for s in s_list]
    denom = functools.reduce(jnp.add, [jnp.sum(e, axis=-1, keepdims=True) for e in e_list])
    outs = [_bdot(e.astype(BF16), v) for e, v in zip(e_list, v_list)]
    return functools.reduce(jnp.add, outs) * (1.0 / denom)


def _qk(q, k):
    return lax.dot_general(q, k, (((1,), (1,)), ((), ())), preferred_element_type=F32)


def _ctx_attn_kernel(q_ref, k_ref, v_ref, o_ref, *, scale):
    for h in range(N_HEADS):
        sl = slice(h * HEAD_DIM, (h + 1) * HEAD_DIM)
        q = q_ref[:, sl].astype(BF16)
        k = k_ref[:, sl].astype(BF16)
        v = v_ref[:, sl].astype(BF16)
        s = _qk(q, k) * scale
        o_ref[:, sl] = _softmax_pv([s], [v]).astype(o_ref.dtype)


def _ctx_attention(q, new_k, new_v, l, out_rows):
    n_seq, _, seq, W = new_k.shape
    kern = functools.partial(_ctx_attn_kernel, scale=HEAD_DIM ** -0.5)
    kv_spec = pl.BlockSpec((None, None, seq, W), lambda b: (b, l, 0, 0))
    return pl.pallas_call(
        kern,
        grid=(n_seq,),
        in_specs=[pl.BlockSpec((seq, W), lambda b: (b, 0)), kv_spec, kv_spec],
        out_specs=pl.BlockSpec((seq, W), lambda b: (b, 0)),
        out_shape=jax.ShapeDtypeStruct((out_rows, W), BF16),
        compiler_params=_params("arbitrary"),
        name="ctx_attention",
    )(q, new_k, new_v)


def _window_starts(rows):
    kr = min(NA_ROWS, rows)
    return [min(max(r - kr // 2, 0), rows - kr) for r in range(rows)], kr


def _row_groups(rows):
    starts, kr = _window_starts(rows)
    groups = []
    for r0 in range(0, rows, NA_GROUP_ROWS):
        r1 = min(r0 + NA_GROUP_ROWS, rows)
        groups.append((r0, r1, min(starts[r0:r1]), max(starts[r0:r1]) + kr))
    return groups


def _na_attn_kernel(att_ref, q_ref, k_ref, v_ref, kc_ref, vc_ref, *rest, scale, rows):
    bias_refs, o_ref = rest[:-1], rest[-1]
    q = q_ref[...].astype(BF16)
    k = k_ref[...].astype(BF16)
    v = v_ref[...].astype(BF16)
    vc = vc_ref[...].astype(BF16)
    s_ctx = _qk(q, kc_ref[...].astype(BF16)) * scale
    for (r0, r1, k0, k1), bias_ref in zip(_row_groups(rows), bias_refs):
        qs = slice(r0 * GRID_W, r1 * GRID_W)
        ks = slice(k0 * GRID_W, k1 * GRID_W)
        s_lat = _qk(q[qs], k[ks]) * scale + bias_ref[...]
        o_ref[qs, :] = _softmax_pv([s_lat, s_ctx[qs]], [v[ks], vc]).astype(o_ref.dtype)


def _na_attention(att, q, kv, row0, n_seq, seq, cache_k, cache_v, biases, l):
    P = cache_k.shape[2]
    r0 = row0 // seq
    rows = seq // GRID_W
    kern = functools.partial(_na_attn_kernel, scale=HEAD_DIM ** -0.5, rows=rows)

    def kvspec(j):
        return pl.BlockSpec((seq, HEAD_DIM), lambda h, b: (b, j * N_HEADS + h))

    cspec = pl.BlockSpec((None, None, P, HEAD_DIM), lambda h, b: (b, l, 0, h))
    bspecs = [pl.BlockSpec((None, None) + bias.shape[2:], lambda h, b: (l, h, 0, 0)) for bias in biases]
    return pl.pallas_call(
        kern,
        grid=(N_HEADS, n_seq),
        in_specs=[pl.BlockSpec(memory_space=pl.ANY),
                  pl.BlockSpec((seq, HEAD_DIM), lambda h, b: (r0 + b, h)),
                  kvspec(0), kvspec(1), cspec, cspec] + bspecs,
        out_specs=pl.BlockSpec((seq, HEAD_DIM), lambda h, b: (r0 + b, h)),
        out_shape=jax.ShapeDtypeStruct(att.shape, att.dtype),
        input_output_aliases={0: 0},
        compiler_params=_params("arbitrary", "arbitrary"),
        name="na_attention",
    )(att, q, kv, kv, cache_k, cache_v, *biases)


def _na_bias(rpb, rows):
    starts, kr = _window_starts(rows)
    col = np.arange(GRID_W)
    cs = np.clip(col - NA_COLS // 2, 0, GRID_W - NA_COLS)
    col_ok = (col[None, :] >= cs[:, None]) & (col[None, :] < cs[:, None] + NA_COLS)
    col_idx = np.clip(col[None, :] - col[:, None] + NA_COLS - 1, 0, 2 * NA_COLS - 2)
    onehot = (col_idx[None] == np.arange(2 * NA_COLS - 1)[:, None, None]) & col_ok[None]
    e = jnp.einsum('lhrd,dqk->lhqrk', rpb.astype(F32), jnp.asarray(onehot, F32),
                   precision=lax.Precision.HIGHEST)
    e = jnp.where(jnp.asarray(col_ok)[:, None, :], e, NEG_INF)
    biases = []
    for r0, r1, k0, k1 in _row_groups(rows):
        per_row = []
        for r in range(r0, r1):
            d0 = starts[r] - r + NA_ROWS - 1
            band = e[:, :, :, d0:d0 + kr]
            band = band.reshape(band.shape[:3] + (kr * GRID_W,))
            pad = ((starts[r] - k0) * GRID_W, (k1 - starts[r] - kr) * GRID_W)
            per_row.append(jnp.pad(band, ((0, 0), (0, 0), (0, 0), pad), constant_values=NEG_INF))
        biases.append(jnp.concatenate(per_row, axis=2))
    return biases


def _split_bf16(x):
    hi = x.astype(BF16)
    return hi, (x - hi.astype(F32)).astype(BF16)


def _dot_split(a, b):
    (ah, al), (bh, bl) = a, b
    return _bdot(ah, bh) + _bdot(ah, bl) + _bdot(al, bh)


def _fnet_kernel(x_ref, wch_ref, wcl_ref, wth_ref, wtl_ref, *rest, gdim):
    o_ref = rest[-1]
    wc = (wch_ref[...], wcl_ref[...])
    wt = (wth_ref[...], wtl_ref[...])
    for c in range(0, x_ref.shape[1], gdim):
        t = _dot_split(_split_bf16(x_ref[:, c:c + gdim]), wc)
        u = jnp.concatenate([t[:, :gdim], t[:, gdim:]], axis=0)
        o_ref[:, c:c + gdim] = _dot_split(wt, _split_bf16(u)).astype(o_ref.dtype)


def _dft_mats(n):
    k = np.arange(n, dtype=np.int64)
    ang = 2.0 * np.pi * ((k[:, None] * k[None, :]) % n) / n
    s = 1.0 / np.sqrt(n)
    return np.cos(ang) * s, np.sin(ang) * s


def _split_const(w):
    w = jnp.asarray(w, F32)
    return _split_bf16(w)


def _fnet(z, col0, groups, gdim, row0, n_seq, seq, prev=None):
    ct, st = _dft_mats(seq)
    cc, sc = _dft_mats(gdim)
    wch, wcl = _split_const(np.concatenate([cc, sc], axis=1))
    wth, wtl = _split_const(np.concatenate([ct, -st], axis=1))
    r0 = row0 // seq
    width = groups * gdim
    c0 = col0 // width

    def const(w):
        return pl.BlockSpec(w.shape, lambda b: (0, 0))

    in_specs = [pl.BlockSpec((seq, width), lambda b: (r0 + b, c0)),
                const(wch), const(wcl), const(wth), const(wtl)]
    args = [z, wch, wcl, wth, wtl]
    aliases = {}
    if prev is not None:
        in_specs.append(pl.BlockSpec(memory_space=pl.ANY))
        args.append(prev)
        aliases = {5: 0}
    return pl.pallas_call(
        functools.partial(_fnet_kernel, gdim=gdim),
        grid=(n_seq,),
        in_specs=in_specs,
        out_specs=pl.BlockSpec((seq, width), lambda b: (r0 + b, 0)),
        out_shape=jax.ShapeDtypeStruct((z.shape[0], width), BF16),
        input_output_aliases=aliases,
        compiler_params=_params("arbitrary"),
        name="fnet",
    )(*args)


def kernel(x_prompt, x_sample, cache_k, cache_v, c, c_ctx, w_mod, b_mod, g_pre1, g_post1, g_pre2, g_post2,
           w_in, rpb, w_na_out, w_fnet_out, conv_w, conv_b, w_conv_out, w_o, w_up, ffn_conv_w, ffn_conv_b,
           w_down):
    B, S, D = x_prompt.shape
    Bd, T, _ = x_sample.shape
    L = w_mod.shape[0]
    P = cache_k.shape[2]
    W = N_HEADS * HEAD_DIM
    n_ctx = B * S
    n_lat = Bd * T
    fw = w_fnet_out.shape[1]
    cwid = w_conv_out.shape[1]
    fgroups = 4
    assert T == ROW_GROUP and n_ctx % ROW_GROUP == 0 and Bd + 1 <= 8

    c_rows = jnp.zeros((8, D), F32).at[:Bd].set(c).at[Bd].set(c_ctx)
    mod = _modulation(c_rows, w_mod, b_mod)
    group_row = np.concatenate([np.full(n_ctx // ROW_GROUP, Bd), np.arange(Bd)])
    mods = mod.reshape(L, 8, 6, D)[:, group_row]
    mods = mods.transpose(0, 2, 1, 3)[:, :, :, None, :]

    gains = [g.reshape(L, 1, D) for g in (g_pre1, g_post1, g_pre2, g_post2)]
    ck = cache_k.reshape(Bd, L, P, W)
    cv_cache = cache_v.reshape(Bd, L, P, W)

    col_f = 3 * W
    col_gate = col_f + fw + 3 * cwid
    biases = _na_bias(rpb, T // GRID_W)

    new_kv = None
    x, h = _prenorm(x_prompt.reshape(n_ctx, D), x_sample.reshape(n_lat, D), gains[0], mods, 0, 1, 0)
    for l in range(L):
        tm, tn = PROJ_TM, PROJ_TN
        q, new_kv, kv_lat, fxbc, _ = _proj_in(h, w_in, l, new_kv, B, S, L, col_gate - col_f, col_gate, tm, tn)
        zg = _matmul(h, w_in, l, BF16, tm, 2 * tn, "proj_gates", col_gate, single_buffer=True)
        new_k, new_v = new_kv
        att = _ctx_attention(q, new_k, new_v, l, n_ctx + n_lat)
        att = _na_attention(att, q, kv_lat, n_ctx, Bd, T, ck, cv_cache, biases, l)
        fr = _fnet(fxbc, 0, fgroups, fw // fgroups, 0, B, S)
        fr = _fnet(fxbc, 0, fgroups, fw // fgroups, n_ctx, Bd, T, prev=fr)
        cv = _sconv(fxbc, fw, cwid, conv_w, conv_b, l, n_ctx, S, T)
        merged = _merge(att, fr, cv, w_na_out, w_fnet_out, w_conv_out, zg, l)
        y = _matmul(merged, w_o, l, F32, tm, tn, "proj_o")
        x, h = _resid(x, y, gains[1], mods, l, 2, (gains[2], l, 4, 3))
        act = _ffn_up(h, w_up, ffn_conv_w, ffn_conv_b, l, n_ctx, S, T)
        y = _matmul(act, w_down, l, F32, 1024, 256, "ffn_down", single_buffer=True)
        if l + 1 < L:
            x, h = _resid(x, y, gains[3], mods, l, 5, (gains[0], l + 1, 1, 0))

    y_prompt = _resid_last(x, y, gains[3], mods, L - 1, 5, 0, n_ctx).reshape(B, S, D)
    y_sample = _resid_last(x, y, gains[3], mods, L - 1, 5, n_ctx, n_lat).reshape(Bd, T, D)
    kv_shape = (B, L, S, N_HEADS, HEAD_DIM)
    return y_prompt, y_sample, new_k.reshape(kv_shape), new_v.reshape(kv_shape)
```
